```python
import jax, jax.numpy as jnp
from jax import lax
import numpy as np

D_MODEL = 2048
BATCH = 4
SEQ = 2048
DEPTH = 2
DEC_BATCH = 8
DEC_SEQ = 1
PAST_LEN = 16384
PAGE_SIZE = 128

RET_HEADS = 8
RET_DK = 128
RET_DV = 128
RET_WK = RET_HEADS * RET_DK
RET_WV = RET_HEADS * RET_DV
RET_CHUNK = 128
CONV_WIDTH = 1024
CONV_K = 3
SB_HEADS = 8
SB_DH = 128
SB_W = SB_HEADS * SB_DH
SB_BLOCK = 128
SB_BIAS_HI = -5.0
SB_BIAS_LO = -10.0
D_FF = 5632
P_DIM = 256
ROPE_THETA = 10000.0
EPS = 1e-6
IN_SPLITS = (RET_WK, RET_WK, RET_WV, CONV_WIDTH, CONV_WIDTH, CONV_WIDTH, SB_W, SB_W, SB_W, D_MODEL, D_MODEL, D_MODEL)
N_IN = RET_WK * 2 + RET_WV + CONV_WIDTH * 3 + SB_W * 3 + D_MODEL * 3

kernel_name = "hybrid_retention_shortconv_stickbreaking_decode_step"


def rmsnorm(x, g):
    xf = x.astype(jnp.float32)
    y = xf * lax.rsqrt(jnp.mean(xf * xf, axis=-1, keepdims=True) + EPS)
    return (y * g.astype(jnp.float32)).astype(x.dtype)


def swiglu(x, w_gu, w_down):
    g, u = jnp.split(x @ w_gu, 2, axis=-1)
    return (jax.nn.silu(g) * u) @ w_down


def rope(x, pos):
    half = x.shape[-1] // 2
    inv = ROPE_THETA ** (-jnp.arange(half, dtype=jnp.float32) / half)
    ang = pos.astype(jnp.float32)[:, None] * inv[None, :]
    cos = jnp.cos(ang)[None, :, None, :]
    sin = jnp.sin(ang)[None, :, None, :]
    xf = x.astype(jnp.float32)
    x1, x2 = xf[..., :half], xf[..., half:]
    return jnp.concatenate([x1 * cos - x2 * sin, x2 * cos + x1 * sin], axis=-1)


def split_in(h):
    parts, start = [], 0
    for width in IN_SPLITS:
        parts.append(h[..., start:start + width])
        start += width
    return parts


def ret_log_gamma():
    return jnp.log1p(-jnp.exp2(-5.0 - jnp.arange(RET_HEADS, dtype=jnp.float32)))


def retention_chunk(s, qkv):
    q, k, v = qkv
    c = q.shape[1]
    lg = ret_log_gamma()
    i = jnp.arange(c, dtype=jnp.float32)
    diff = i[:, None] - i[None, :]
    dmask = jnp.where(diff[None] >= 0, jnp.exp(lg[:, None, None] * jnp.maximum(diff, 0.0)[None]), 0.0)
    scores = jnp.einsum('bihd,bjhd->bhij', q, k) * dmask[None]
    inner = jnp.einsum('bhij,bjhe->bihe', scores, v)
    q_decay = jnp.exp(lg[None, :] * (i + 1.0)[:, None])
    cross = jnp.einsum('bihd,bhde->bihe', q, s) * q_decay[None, :, :, None]
    k_decay = jnp.exp(lg[None, :] * (c - 1.0 - i)[:, None])
    s_new = jnp.exp(lg * c)[None, :, None, None] * s + jnp.einsum('bjhd,bjhe->bhde', k * k_decay[None, :, :, None], v)
    return s_new, inner + cross


def retention(q, k, v, s0):
    b, l = q.shape[:2]
    c = RET_CHUNK if l % RET_CHUNK == 0 else l
    nc = l // c

    def to_chunks(t):
        return t.reshape(b, nc, c, *t.shape[2:]).swapaxes(0, 1)

    s, o = lax.scan(retention_chunk, s0, (to_chunks(q), to_chunks(k), to_chunks(v)))
    return o.swapaxes(0, 1).reshape(b, l, RET_HEADS, RET_DV), s


def short_conv(u, prev, w):
    ext = jnp.concatenate([prev.astype(u.dtype), u], axis=1)
    l = u.shape[1]
    out = sum(ext[:, j:j + l] * w[j] for j in range(CONV_K))
    return out, ext[:, -(CONV_K - 1):]


def sb_block(q, k, v, q_pos, k_pos, bias):
    z = jnp.einsum('bqhd,bkhd->bhqk', q, k) * (SB_DH ** -0.5) + bias.astype(jnp.float32)[None, :, None, None]
    mask = (k_pos[None, :] < q_pos[:, None])[None, None]
    log_1mb = jnp.where(mask, jax.nn.log_sigmoid(-z), 0.0)
    after = lax.cumsum(log_1mb, axis=3, reverse=True) - log_1mb
    a = jnp.where(mask, jnp.exp(jax.nn.log_sigmoid(z) + after), 0.0)
    return jnp.einsum('bhqk,bkhd->bqhd', a, v)


def stick_breaking(q, k, v, q_pos, k_pos, bias):
    b, lq = q.shape[:2]
    t = SB_BLOCK if lq % SB_BLOCK == 0 else lq
    nb = lq // t
    qb = q.reshape(b, nb, t, SB_HEADS, SB_DH).swapaxes(0, 1)
    pb = q_pos.reshape(nb, t)
    o = lax.map(lambda a: sb_block(a[0], k, v, a[1], k_pos, bias), (qb, pb))
    return o.swapaxes(0, 1).reshape(b, lq, SB_HEADS, SB_DH)


def trunk_layer(x, p, offset, s_ret, conv_prev, past_k, past_v, w):
    b, l, _ = x.shape
    f32 = jnp.float32
    pos = offset + jnp.arange(l, dtype=jnp.int32)
    x = x + 0.5 * swiglu(rmsnorm(x, w['ffn1_norm']), w['ffn1_w_gu'], w['ffn1_w_down'])
    xn = rmsnorm(x, w['mix_norm'])
    rq, rk, rv, ch, cb, cc, sq, sk, sv, ga, gb, gc = split_in(xn @ w['w_in'])
    rq = rope(rq.reshape(b, l, RET_HEADS, RET_DK), pos)
    rk = rope(rk.reshape(b, l, RET_HEADS, RET_DK), pos) * (RET_DK ** -0.5)
    rv = rv.reshape(b, l, RET_HEADS, RET_DV).astype(f32)
    ro, s_new = retention(rq, rk, rv, s_ret.astype(f32))
    ya = rmsnorm(ro, w['ret_gn']).reshape(b, l, RET_WV).astype(x.dtype)
    cv, conv_new = short_conv(cc * ch, conv_prev, w['conv_w'])
    yb = cb * cv
    sq = rmsnorm(sq.reshape(b, l, SB_HEADS, SB_DH), w['sb_q_norm'])
    sk = rmsnorm(sk.reshape(b, l, SB_HEADS, SB_DH), w['sb_k_norm'])
    sv = sv.reshape(b, l, SB_HEADS, SB_DH)
    if past_k is None:
        k_all, v_all = sk, sv
    else:
        k_all = jnp.concatenate([past_k.astype(sk.dtype), sk], axis=1)
        v_all = jnp.concatenate([past_v.astype(sv.dtype), sv], axis=1)
    k_pos = jnp.arange(k_all.shape[1], dtype=jnp.int32)
    yc = stick_breaking(sq.astype(f32), k_all.astype(f32), v_all.astype(f32), pos, k_pos, w['sb_bias'])
    yc = yc.reshape(b, l, SB_W).astype(x.dtype)
    merged = (jax.nn.sigmoid(ga) * (ya @ w['w_branch_ret'])
              + jax.nn.sigmoid(gb) * (yb @ w['w_branch_conv'])
              + jax.nn.sigmoid(gc) * (yc @ w['w_branch_sb']))
    x = x + merged @ w['w_out']
    x = x + 0.5 * swiglu(rmsnorm(x, w['ffn2_norm']), w['ffn2_w_gu'], w['ffn2_w_down'])
    x = x + jax.nn.sigmoid(rmsnorm(x, w['ple_norm']) @ w['w_ple_gate']) * (p @ w['w_ple_up'])
    return x, s_new, conv_new, sk, sv


def setup_inputs(seed: int = 0) -> dict:
    key = jax.random.key(seed)
    ks = jax.random.split(key, 32)
    f32 = jnp.float32
    n_pages = PAST_LEN // PAGE_SIZE
    n_used = DEC_BATCH * n_pages
    n_pool = n_used + (n_used + 3) // 4

    def nrm(k, shape, fan_in):
        return jax.random.normal(k, shape, f32) * (fan_in ** -0.5)

    def gain(k, shape):
        return 1.0 + 0.02 * jax.random.normal(k, shape, f32)

    perm = jax.random.permutation(ks[7], n_pool)
    page_table = perm[:n_used].reshape(DEC_BATCH, n_pages).astype(jnp.int32)
    sb_bias = jnp.linspace(SB_BIAS_HI, SB_BIAS_LO, SB_HEADS, dtype=f32)[None, :] + 0.1 * jax.random.normal(ks[28], (DEPTH, SB_HEADS), f32)
    return {
        'x_prompt': jax.random.normal(ks[0], (BATCH, SEQ, D_MODEL), f32),
        'x_sample': jax.random.normal(ks[1], (DEC_BATCH, DEC_SEQ, D_MODEL), f32),
        'cache_sb_k': jax.random.normal(ks[2], (DEPTH, n_pool, PAGE_SIZE, SB_HEADS, SB_DH), f32),
        'cache_sb_v': jax.random.normal(ks[3], (DEPTH, n_pool, PAGE_SIZE, SB_HEADS, SB_DH), f32),
        'state_ret': 0.3 * jax.random.normal(ks[4], (DEPTH, DEC_BATCH, RET_HEADS, RET_DK, RET_DV), f32),
        'state_conv': jax.random.normal(ks[5], (DEPTH, DEC_BATCH, CONV_K - 1, CONV_WIDTH), f32),
        'page_table': page_table,
        'p_prompt': jax.random.normal(ks[6], (DEPTH, BATCH, SEQ, P_DIM), f32),
        'p_sample': jax.random.normal(ks[8], (DEPTH, DEC_BATCH, DEC_SEQ, P_DIM), f32),
        'ffn1_norm': gain(ks[9], (DEPTH, D_MODEL)),
        'ffn1_w_gu': nrm(ks[10], (DEPTH, D_MODEL, 2 * D_FF), D_MODEL),
        'ffn1_w_down': nrm(ks[11], (DEPTH, D_FF, D_MODEL), D_FF),
        'mix_norm': gain(ks[12], (DEPTH, D_MODEL)),
        'w_in': nrm(ks[13], (DEPTH, D_MODEL, N_IN), D_MODEL),
        'ret_gn': gain(ks[14], (DEPTH, RET_HEADS, RET_DV)),
        'conv_w': nrm(ks[15], (DEPTH, CONV_K, CONV_WIDTH), CONV_K),
        'sb_q_norm': gain(ks[16], (DEPTH, SB_DH)),
        'sb_k_norm': gain(ks[17], (DEPTH, SB_DH)),
        'sb_bias': sb_bias,
        'w_branch_ret': nrm(ks[18], (DEPTH, RET_WV, D_MODEL), RET_WV),
        'w_branch_conv': nrm(ks[19], (DEPTH, CONV_WIDTH, D_MODEL), CONV_WIDTH),
        'w_branch_sb': nrm(ks[20], (DEPTH, SB_W, D_MODEL), SB_W),
        'w_out': nrm(ks[21], (DEPTH, D_MODEL, D_MODEL), D_MODEL),
        'ffn2_norm': gain(ks[22], (DEPTH, D_MODEL)),
        'ffn2_w_gu': nrm(ks[23], (DEPTH, D_MODEL, 2 * D_FF), D_MODEL),
        'ffn2_w_down': nrm(ks[24], (DEPTH, D_FF, D_MODEL), D_FF),
        'ple_norm': gain(ks[25], (DEPTH, D_MODEL)),
        'w_ple_gate': nrm(ks[26], (DEPTH, D_MODEL, D_MODEL), D_MODEL),
        'w_ple_up': nrm(ks[27], (DEPTH, P_DIM, D_MODEL), P_DIM),
    }


def reference(x_prompt, x_sample, cache_sb_k, cache_sb_v, state_ret, state_conv, page_table, p_prompt, p_sample,
              ffn1_norm, ffn1_w_gu, ffn1_w_down, mix_norm, w_in, ret_gn, conv_w, sb_q_norm, sb_k_norm, sb_bias,
              w_branch_ret, w_branch_conv, w_branch_sb, w_out, ffn2_norm, ffn2_w_gu, ffn2_w_down,
              ple_norm, w_ple_gate, w_ple_up):
    n_pages = PAST_LEN // PAGE_SIZE
    b_p = x_prompt.shape[0]
    b_s = x_sample.shape[0]
    yp, ys = x_prompt, x_sample
    kp_l, vp_l, ks_l, vs_l, rp_l, rs_l, cp_l, cs_l = [], [], [], [], [], [], [], []
    for i in range(DEPTH):
        w = {
            'ffn1_norm': ffn1_norm[i], 'ffn1_w_gu': ffn1_w_gu[i], 'ffn1_w_down': ffn1_w_down[i],
            'mix_norm': mix_norm[i], 'w_in': w_in[i], 'ret_gn': ret_gn[i], 'conv_w': conv_w[i],
            'sb_q_norm': sb_q_norm[i], 'sb_k_norm': sb_k_norm[i], 'sb_bias': sb_bias[i],
            'w_branch_ret': w_branch_ret[i], 'w_branch_conv': w_branch_conv[i], 'w_branch_sb': w_branch_sb[i],
            'w_out': w_out[i], 'ffn2_norm': ffn2_norm[i], 'ffn2_w_gu': ffn2_w_gu[i], 'ffn2_w_down': ffn2_w_down[i],
            'ple_norm': ple_norm[i], 'w_ple_gate': w_ple_gate[i], 'w_ple_up': w_ple_up[i],
        }
        s0 = jnp.zeros((b_p, RET_HEADS, RET_DK, RET_DV), jnp.float32)
        c0 = jnp.zeros((b_p, CONV_K - 1, CONV_WIDTH), yp.dtype)
        yp, sp, cp, kp, vp = trunk_layer(yp, p_prompt[i], 0, s0, c0, None, None, w)
        past_k = cache_sb_k[i][page_table].reshape(b_s, n_pages * PAGE_SIZE, SB_HEADS, SB_DH)
        past_v = cache_sb_v[i][page_table].reshape(b_s, n_pages * PAGE_SIZE, SB_HEADS, SB_DH)
        ys, ss, cs, ksm, vsm = trunk_layer(ys, p_sample[i], PAST_LEN, state_ret[i], state_conv[i], past_k, past_v, w)
        kp_l.append(kp); vp_l.append(vp); ks_l.append(ksm); vs_l.append(vsm)
        rp_l.append(sp); rs_l.append(ss); cp_l.append(cp); cs_l.append(cs)
    return (yp, ys, jnp.stack(kp_l), jnp.stack(vp_l), jnp.stack(ks_l), jnp.stack(vs_l),
            jnp.stack(rp_l), jnp.stack(rs_l), jnp.stack(cp_l), jnp.stack(cs_l))
```

```python
import functools

import jax
import jax.numpy as jnp
from jax import lax
from jax.experimental import pallas as pl
from jax.experimental.pallas import tpu as pltpu

F32 = jnp.float32
BF16 = jnp.bfloat16

D_MODEL = 2048
DEPTH = 2
PAGE_SIZE = 128
HEADS = 8
HEAD_DIM = 128
MIX_W = HEADS * HEAD_DIM
CHUNK = 128
D_FF = 5632
P_DIM = 256
ROPE_THETA = 10000.0
EPS = 1e-6
N_IN = 9 * MIX_W + 3 * D_MODEL
COL_RQ, COL_RK, COL_RV, COL_CH, COL_CB, COL_CC, COL_SQ, COL_SK, COL_SV = range(9)
COL_GA, COL_GB, COL_GC = 9, 11, 13

VMEM_LIMIT_BYTES = 56 * 1024 * 1024


def _params(*sem):
    return pltpu.CompilerParams(dimension_semantics=sem, vmem_limit_bytes=VMEM_LIMIT_BYTES)


def _dot(a, b):
    return jnp.dot(a, b, preferred_element_type=F32)


def _dot_nt(a, b):
    return lax.dot_general(a, b, (((1,), (1,)), ((), ())), preferred_element_type=F32)


def _dot_tn(a, b):
    return lax.dot_general(a, b, (((0,), (0,)), ((), ())), preferred_element_type=F32)


def _rms(x, g):
    return x * lax.rsqrt(jnp.mean(x * x, axis=-1, keepdims=True) + EPS) * g


def _sigmoid(x):
    return jax.nn.sigmoid(x)


def _rmsnorm_kernel(x_ref, g_ref, o_ref):
    o_ref[...] = _rms(x_ref[...], g_ref[...]).astype(o_ref.dtype)


def rmsnorm_bf16(x, g, layer, tm):
    m, d = x.shape
    return pl.pallas_call(
        _rmsnorm_kernel,
        grid=(m // tm,),
        in_specs=[pl.BlockSpec((tm, d), lambda i: (i, 0)),
                  pl.BlockSpec((None, 1, d), lambda i: (layer, 0, 0))],
        out_specs=pl.BlockSpec((tm, d), lambda i: (i, 0)),
        out_shape=jax.ShapeDtypeStruct((m, d), BF16),
        compiler_params=_params("parallel"),
        name="rmsnorm",
    )(x, g.reshape(DEPTH, 1, d))


def _swiglu_kernel(a_ref, wg_ref, wu_ref, o_ref):
    a = a_ref[...]
    g = _dot(a, wg_ref[...])
    u = _dot(a, wu_ref[...])
    o_ref[...] = (g * _sigmoid(g) * u).astype(o_ref.dtype)


def swiglu_mm(a, w_gu, layer, tm, tn):
    m, k = a.shape
    nf = D_FF // tn
    return pl.pallas_call(
        _swiglu_kernel,
        grid=(nf, m // tm),
        in_specs=[pl.BlockSpec((tm, k), lambda j, i: (i, 0)),
                  pl.BlockSpec((None, k, tn), lambda j, i: (layer, 0, j)),
                  pl.BlockSpec((None, k, tn), lambda j, i: (layer, 0, j + nf))],
        out_specs=pl.BlockSpec((tm, tn), lambda j, i: (i, j)),
        out_shape=jax.ShapeDtypeStruct((m, D_FF), BF16),
        compiler_params=_params("parallel", "parallel"),
        name="swiglu_mm",
    )(a, w_gu, w_gu)


def _mm_res_kernel(a_ref, w_ref, r_ref, o_ref, *, scale):
    o_ref[...] = r_ref[...] + scale * _dot(a_ref[...], w_ref[...])


def mm_residual(a, w, res, layer, scale, tm, tn):
    m, k = a.shape
    n = w.shape[-1]
    return pl.pallas_call(
        functools.partial(_mm_res_kernel, scale=scale),
        grid=(n // tn, m // tm),
        in_specs=[pl.BlockSpec((tm, k), lambda j, i: (i, 0)),
                  pl.BlockSpec((None, k, tn), lambda j, i: (layer, 0, j)),
                  pl.BlockSpec((tm, tn), lambda j, i: (i, j))],
        out_specs=pl.BlockSpec((tm, tn), lambda j, i: (i, j)),
        out_shape=jax.ShapeDtypeStruct((m, n), F32),
        compiler_params=_params("parallel", "parallel"),
        name="mm_residual",
    )(a, w, res)


def _mm_kernel(a_ref, w_ref, o_ref):
    o_ref[...] = _dot(a_ref[...], w_ref[...])


def mm_f32(a, w, layer, tm, tn):
    m, k = a.shape
    n = w.shape[-1]
    return pl.pallas_call(
        _mm_kernel,
        grid=(n // tn, m // tm),
        in_specs=[pl.BlockSpec((tm, k), lambda j, i: (i, 0)),
                  pl.BlockSpec((None, k, tn), lambda j, i: (layer, 0, j))],
        out_specs=pl.BlockSpec((tm, tn), lambda j, i: (i, j)),
        out_shape=jax.ShapeDtypeStruct((m, n), F32),
        compiler_params=_params("parallel", "parallel"),
        name="mm_in_proj",
    )(a, w)


def _merge_kernel(ya_ref, yb_ref, yc_ref, wr_ref, wc_ref, ws_ref, ga_ref, gb_ref, gc_ref, o_ref):
    m = (_sigmoid(ga_ref[...]) * _dot(ya_ref[...], wr_ref[...])
         + _sigmoid(gb_ref[...]) * _dot(yb_ref[...], wc_ref[...])
         + _sigmoid(gc_ref[...]) * _dot(yc_ref[...], ws_ref[...]))
    o_ref[...] = m.astype(o_ref.dtype)


def merge_branches(ya, yb, yc, w_ret, w_conv, w_sb, proj, layer, tm):
    m = ya.shape[0]
    tn = MIX_W
    y_spec = pl.BlockSpec((tm, MIX_W), lambda j, i: (i, 0))
    w_spec = pl.BlockSpec((None, MIX_W, tn), lambda j, i: (layer, 0, j))

    def gate_spec(col):
        return pl.BlockSpec((tm, tn), lambda j, i: (i, col + j))

    return pl.pallas_call(
        _merge_kernel,
        grid=(D_MODEL // tn, m // tm),
        in_specs=[y_spec, y_spec, y_spec, w_spec, w_spec, w_spec,
                  gate_spec(COL_GA), gate_spec(COL_GB), gate_spec(COL_GC)],
        out_specs=pl.BlockSpec((tm, tn), lambda j, i: (i, j)),
        out_shape=jax.ShapeDtypeStruct((m, D_MODEL), BF16),
        compiler_params=_params("parallel", "parallel"),
        name="merge_branches",
    )(ya, yb, yc, w_ret, w_conv, w_sb, proj, proj, proj)


def _ple_kernel(h_ref, wg_ref, p_ref, wu_ref, x_ref, o_ref):
    gate = _dot(h_ref[...], wg_ref[...])
    up = _dot(p_ref[...].astype(BF16), wu_ref[...])
    o_ref[...] = x_ref[...] + _sigmoid(gate) * up


def ple_update(h, w_gate, p, w_up, x, layer, tm, tn):
    m = h.shape[0]
    return pl.pallas_call(
        _ple_kernel,
        grid=(D_MODEL // tn, m // tm),
        in_specs=[pl.BlockSpec((tm, D_MODEL), lambda j, i: (i, 0)),
                  pl.BlockSpec((None, D_MODEL, tn), lambda j, i: (layer, 0, j)),
                  pl.BlockSpec((tm, P_DIM), lambda j, i: (i, 0)),
                  pl.BlockSpec((None, P_DIM, tn), lambda j, i: (layer, 0, j)),
                  pl.BlockSpec((tm, tn), lambda j, i: (i, j))],
        out_specs=pl.BlockSpec((tm, tn), lambda j, i: (i, j)),
        out_shape=jax.ShapeDtypeStruct((m, D_MODEL), F32),
        compiler_params=_params("parallel", "parallel"),
        name="ple_update",
    )(h, w_gate, p, w_up, x)


def _rope(x, cos2, sin2):
    return x * cos2 + pltpu.roll(x, HEAD_DIM // 2, 1) * sin2


def _softplus_neg_abs(z):
    return jnp.log1p(jnp.exp(-jnp.abs(z)))


def _suffix_sum_exclusive(lg, tri):
    hi = lg.astype(BF16)
    lo = (lg - hi.astype(F32)).astype(BF16)
    return _dot(hi, tri) + _dot(lo, tri)


def _retention_kernel(lg_ref, q_ref, k_ref, v_ref, cos_ref, sin_ref, gn_ref, ya_ref, s_out_ref, s_scr,
                      *, n_chunks):
    c = pl.program_id(1)

    @pl.when(c == 0)
    def _():
        s_scr[...] = jnp.zeros_like(s_scr)

    cos2 = cos_ref[...]
    sin2 = sin_ref[...]
    ii = lax.broadcasted_iota(jnp.int32, (CHUNK, CHUNK), 0)
    jj = lax.broadcasted_iota(jnp.int32, (CHUNK, CHUNK), 1)
    diff = (ii - jj).astype(F32)
    pos = lax.broadcasted_iota(jnp.int32, (CHUNK, 1), 0).astype(F32)
    for h in range(HEADS):
        cols = slice(h * HEAD_DIM, (h + 1) * HEAD_DIM)
        lg = lg_ref[h]
        q = _rope(q_ref[:, cols], cos2, sin2)
        k = _rope(k_ref[:, cols], cos2, sin2) * (HEAD_DIM ** -0.5)
        v = v_ref[:, cols].astype(BF16)
        qb = q.astype(BF16)
        dmask = jnp.where(diff >= 0, jnp.exp(lg * jnp.maximum(diff, 0.0)), 0.0)
        scores = _dot_nt(qb, k.astype(BF16)) * dmask
        inner = _dot(scores.astype(BF16), v)
        s = s_scr[h]
        cross = _dot(qb, s.astype(BF16)) * jnp.exp(lg * (pos + 1.0))
        kd = (k * jnp.exp(lg * (CHUNK - 1.0 - pos))).astype(BF16)
        s_scr[h] = jnp.exp(lg * CHUNK) * s + _dot_tn(kd, v)
        o = inner + cross
        ya_ref[:, cols] = _rms(o, gn_ref[h:h + 1, :]).astype(ya_ref.dtype)

    @pl.when(c == n_chunks - 1)
    def _():
        s_out_ref[...] = s_scr[...]


def retention_prompt(proj, cos2, sin2, lg, gn, layer, batch, seq):
    n_chunks = seq // CHUNK
    m = batch * seq

    def col_spec(col):
        return pl.BlockSpec((CHUNK, MIX_W), lambda b, c, lg_ref: (b * n_chunks + c, col))

    tab_spec = pl.BlockSpec((CHUNK, HEAD_DIM), lambda b, c, lg_ref: (c, 0))
    return pl.pallas_call(
        functools.partial(_retention_kernel, n_chunks=n_chunks),
        grid_spec=pltpu.PrefetchScalarGridSpec(
            num_scalar_prefetch=1,
            grid=(batch, n_chunks),
            in_specs=[col_spec(COL_RQ), col_spec(COL_RK), col_spec(COL_RV), tab_spec, tab_spec,
                      pl.BlockSpec((None, HEADS, HEAD_DIM), lambda b, c, lg_ref: (layer, 0, 0))],
            out_specs=[pl.BlockSpec((CHUNK, MIX_W), lambda b, c, lg_ref: (b * n_chunks + c, 0)),
                       pl.BlockSpec((None, HEADS, HEAD_DIM, HEAD_DIM), lambda b, c, lg_ref: (b, 0, 0, 0))],
            scratch_shapes=[pltpu.VMEM((HEADS, HEAD_DIM, HEAD_DIM), F32)]),
        out_shape=[jax.ShapeDtypeStruct((m, MIX_W), BF16),
                   jax.ShapeDtypeStruct((batch, HEADS, HEAD_DIM, HEAD_DIM), F32)],
        compiler_params=_params("parallel", "arbitrary"),
        name="retention_prompt",
    )(lg, proj, proj, proj, cos2, sin2, gn)


def _conv_kernel(ch_ref, cb_ref, cc_ref, w_ref, yb_ref, cn_ref):
    u = cc_ref[...] * ch_ref[...]
    seq = u.shape[0]
    t = lax.broadcasted_iota(jnp.int32, u.shape, 0)
    u1 = jnp.where(t >= 1, pltpu.roll(u, 1, 0), 0.0)
    u2 = jnp.where(t >= 2, pltpu.roll(u, 2, 0), 0.0)
    cv = u2 * w_ref[0:1, :] + u1 * w_ref[1:2, :] + u * w_ref[2:3, :]
    yb_ref[...] = (cb_ref[...] * cv).astype(yb_ref.dtype)
    cn_ref[...] = cc_ref[seq - 2:seq, :] * ch_ref[seq - 2:seq, :]


def conv_prompt(proj, conv_w, layer, batch, seq):
    tw = 256
    nw = MIX_W // tw

    def col_spec(col):
        return pl.BlockSpec((seq, tw), lambda b, j: (b, col * nw + j))

    return pl.pallas_call(
        _conv_kernel,
        grid=(batch, nw),
        in_specs=[col_spec(COL_CH), col_spec(COL_CB), col_spec(COL_CC),
                  pl.BlockSpec((None, 3, tw), lambda b, j: (layer, 0, j))],
        out_specs=[pl.BlockSpec((seq, tw), lambda b, j: (b, j)),
                   pl.BlockSpec((None, 2, tw), lambda b, j: (b, 0, j))],
        out_shape=[jax.ShapeDtypeStruct((batch * seq, MIX_W), BF16),
                   jax.ShapeDtypeStruct((batch, 2, MIX_W), F32)],
        compiler_params=_params("parallel", "parallel"),
        name="conv_prompt",
    )(proj, proj, proj, conv_w)


def _head_norm_kernel(x_ref, g_ref, o_ref):
    g = g_ref[...]
    for h in range(HEADS):
        cols = slice(h * HEAD_DIM, (h + 1) * HEAD_DIM)
        o_ref[:, cols] = _rms(x_ref[:, cols], g)


def head_norm(proj, g, col, layer, tm):
    m = proj.shape[0]
    return pl.pallas_call(
        _head_norm_kernel,
        grid=(m // tm,),
        in_specs=[pl.BlockSpec((tm, MIX_W), lambda i: (i, col)),
                  pl.BlockSpec((None, 1, HEAD_DIM), lambda i: (layer, 0, 0))],
        out_specs=pl.BlockSpec((tm, MIX_W), lambda i: (i, 0)),
        out_shape=jax.ShapeDtypeStruct((m, MIX_W), F32),
        compiler_params=_params("parallel"),
        name="head_norm",
    )(proj, g.reshape(DEPTH, 1, HEAD_DIM))


def _sb_prompt_kernel(bias_ref, q_ref, k_ref, v_ref, qg_ref, o_ref):
    h = pl.program_id(1)
    qi = pl.program_id(2)
    qn = _rms(q_ref[...], qg_ref[...]).astype(BF16)
    bias = bias_ref[h]
    row = lax.broadcasted_iota(jnp.int32, (CHUNK, CHUNK), 0)
    col = lax.broadcasted_iota(jnp.int32, (CHUNK, CHUNK), 1)
    tri = jnp.where(row > col, 1.0, 0.0).astype(BF16)

    def body(it, carry):
        acc, run = carry
        kj = qi - it
        start = pl.multiple_of(kj * CHUNK, CHUNK)
        kb = k_ref[pl.ds(start, CHUNK), :].astype(BF16)
        vb = v_ref[pl.ds(start, CHUNK), :].astype(BF16)
        z = _dot_nt(qn, kb) * (HEAD_DIM ** -0.5) + bias
        mask = (kj * CHUNK + col) < (qi * CHUNK + row)
        sp = _softplus_neg_abs(z)
        log_b = jnp.minimum(z, 0.0) - sp
        log_1mb = jnp.where(mask, -jnp.maximum(z, 0.0) - sp, 0.0)
        after = _suffix_sum_exclusive(log_1mb, tri)
        a = jnp.where(mask, jnp.exp(log_b + after + run), 0.0)
        acc = acc + _dot(a.astype(BF16), vb)
        run = run + jnp.sum(log_1mb, axis=1, keepdims=True)
        return acc, run

    acc, _ = lax.fori_loop(0, qi + 1, body,
                           (jnp.zeros((CHUNK, HEAD_DIM), F32), jnp.zeros((CHUNK, 1), F32)))
    o_ref[...] = acc.astype(o_ref.dtype)


def sb_prompt(proj, sk, q_gain, bias, layer, batch, seq):
    nq = seq // CHUNK
    return pl.pallas_call(
        _sb_prompt_kernel,
        grid_spec=pltpu.PrefetchScalarGridSpec(
            num_scalar_prefetch=1,
            grid=(batch, HEADS, nq),
            in_specs=[pl.BlockSpec((CHUNK, HEAD_DIM), lambda b, h, qi, s: (b * nq + qi, COL_SQ * HEADS + h)),
                      pl.BlockSpec((seq, HEAD_DIM), lambda b, h, qi, s: (b, h)),
                      pl.BlockSpec((seq, HEAD_DIM), lambda b, h, qi, s: (b, COL_SV * HEADS + h)),
                      pl.BlockSpec((None, 1, HEAD_DIM), lambda b, h, qi, s: (layer, 0, 0))],
            out_specs=pl.BlockSpec((CHUNK, HEAD_DIM), lambda b, h, qi, s: (b * nq + qi, h))),
        out_shape=jax.ShapeDtypeStruct((batch * seq, MIX_W), BF16),
        compiler_params=_params("parallel", "parallel", "parallel"),
        name="sb_prompt",
    )(bias, proj, sk, proj, q_gain.reshape(DEPTH, 1, HEAD_DIM))


def _decode_mix_kernel(lg_ref, p_ref, s_ref, cprev_ref, cos_ref, sin_ref, gn_ref, cw_ref, qg_ref, kg_ref,
                       ya_ref, yb_ref, sq_ref, sk_ref, s_out_ref, cn_ref):
    cos2 = cos_ref[...]
    sin2 = sin_ref[...]
    row = lax.broadcasted_iota(jnp.int32, (HEAD_DIM, HEAD_DIM), 0)
    col = lax.broadcasted_iota(jnp.int32, (HEAD_DIM, HEAD_DIM), 1)

    def stream(c, h):
        return p_ref[:, c * MIX_W + h * HEAD_DIM:c * MIX_W + (h + 1) * HEAD_DIM]

    def bf_round(x):
        return x.astype(BF16).astype(F32)

    for h in range(HEADS):
        cols = slice(h * HEAD_DIM, (h + 1) * HEAD_DIM)
        lg = lg_ref[h]
        q = bf_round(_rope(stream(COL_RQ, h), cos2, sin2))
        k = bf_round(_rope(stream(COL_RK, h), cos2, sin2) * (HEAD_DIM ** -0.5))
        v = bf_round(stream(COL_RV, h))
        s = s_ref[h]
        score = bf_round(jnp.sum(q * k, axis=-1, keepdims=True))
        inner = score * v
        q_rows = jnp.broadcast_to(q, (16, HEAD_DIM)).astype(BF16)
        cross = _dot(q_rows, s.astype(BF16))[0:1, :] * jnp.exp(lg * jnp.ones((1, 1), F32))
        k_col = jnp.sum(jnp.where(row == col, jnp.broadcast_to(k, (HEAD_DIM, HEAD_DIM)), 0.0),
                        axis=1, keepdims=True)
        s_out_ref[h] = jnp.exp(lg * jnp.ones((1, 1), F32)) * s + k_col * v
        ya_ref[:, cols] = _rms(inner + cross, gn_ref[h:h + 1, :]).astype(ya_ref.dtype)
        sq_ref[:, cols] = _rms(stream(COL_SQ, h), qg_ref[...])
        sk_ref[:, cols] = _rms(stream(COL_SK, h), kg_ref[...])

    ch = p_ref[:, COL_CH * MIX_W:(COL_CH + 1) * MIX_W]
    cb = p_ref[:, COL_CB * MIX_W:(COL_CB + 1) * MIX_W]
    cc = p_ref[:, COL_CC * MIX_W:(COL_CC + 1) * MIX_W]
    u = cc * ch
    cv = cprev_ref[0:1, :] * cw_ref[0:1, :] + cprev_ref[1:2, :] * cw_ref[1:2, :] + u * cw_ref[2:3, :]
    yb_ref[...] = (cb * cv).astype(yb_ref.dtype)
    cn_ref[0:1, :] = cprev_ref[1:2, :]
    cn_ref[1:2, :] = u


def decode_mix(proj_s, state_ret, state_conv, cos2, sin2, lg, gn, conv_w, q_gain, k_gain, layer, nb):
    def row_spec(width):
        return pl.BlockSpec((None, 1, width), lambda b, s: (b, 0, 0))

    def w_spec(*shape):
        return pl.BlockSpec((None,) + shape, lambda b, s: (layer,) + (0,) * len(shape))

    tab_spec = pl.BlockSpec((1, HEAD_DIM), lambda b, s: (0, 0))
    return pl.pallas_call(
        _decode_mix_kernel,
        grid_spec=pltpu.PrefetchScalarGridSpec(
            num_scalar_prefetch=1,
            grid=(nb,),
            in_specs=[row_spec(N_IN),
                      pl.BlockSpec((None, None, HEADS, HEAD_DIM, HEAD_DIM), lambda b, s: (layer, b, 0, 0, 0)),
                      pl.BlockSpec((None, None, 2, MIX_W), lambda b, s: (layer, b, 0, 0)),
                      tab_spec, tab_spec, w_spec(HEADS, HEAD_DIM), w_spec(3, MIX_W),
                      w_spec(1, HEAD_DIM), w_spec(1, HEAD_DIM)],
            out_specs=[row_spec(MIX_W), row_spec(MIX_W), row_spec(MIX_W), row_spec(MIX_W),
                       pl.BlockSpec((None, HEADS, HEAD_DIM, HEAD_DIM), lambda b, s: (b, 0, 0, 0)),
                       pl.BlockSpec((None, 2, MIX_W), lambda b, s: (b, 0, 0))]),
        out_shape=[jax.ShapeDtypeStruct((nb, 1, MIX_W), BF16),
                   jax.ShapeDtypeStruct((nb, 1, MIX_W), BF16),
                   jax.ShapeDtypeStruct((nb, 1, MIX_W), F32),
                   jax.ShapeDtypeStruct((nb, 1, MIX_W), F32),
                   jax.ShapeDtypeStruct((nb, HEADS, HEAD_DIM, HEAD_DIM), F32),
                   jax.ShapeDtypeStruct((nb, 2, MIX_W), F32)],
        compiler_params=_params("parallel"),
        name="decode_mix",
    )(lg, proj_s, state_ret, state_conv, cos2, sin2, gn, conv_w,
      q_gain.reshape(DEPTH, 1, HEAD_DIM), k_gain.reshape(DEPTH, 1, HEAD_DIM))


Q_ROWS = 16


def _sb_decode_kernel(pt_ref, q_ref, k_ref, v_ref, bias_ref, o_ref, acc_scr, run_scr, *, n_pages):
    j = pl.program_id(1)

    @pl.when(j == 0)
    def _():
        acc_scr[...] = jnp.zeros_like(acc_scr)
        run_scr[...] = jnp.zeros_like(run_scr)

    q = q_ref[...]
    hsel = lax.broadcasted_iota(jnp.int32, (Q_ROWS, HEAD_DIM), 0)
    row = lax.broadcasted_iota(jnp.int32, (PAGE_SIZE, PAGE_SIZE), 0)
    col = lax.broadcasted_iota(jnp.int32, (PAGE_SIZE, PAGE_SIZE), 1)
    tri = jnp.where(row > col, 1.0, 0.0).astype(BF16)

    z = jnp.zeros((Q_ROWS, PAGE_SIZE), F32)
    for h in range(HEADS):
        qs = jnp.where(hsel == h, q, 0.0).astype(BF16)
        z = z + _dot_nt(qs, k_ref[:, h, :].astype(BF16))
    z = z * (HEAD_DIM ** -0.5) + bias_ref[...]
    sp = _softplus_neg_abs(z)
    log_b = jnp.minimum(z, 0.0) - sp
    log_1mb = -jnp.maximum(z, 0.0) - sp
    after = _suffix_sum_exclusive(log_1mb, tri)
    a = jnp.exp(log_b + after + run_scr[...])
    acc = acc_scr[...]
    for h in range(HEADS):
        a_h = jnp.where(hsel == h, a, 0.0).astype(BF16)
        acc = acc + _dot(a_h, v_ref[:, h, :].astype(BF16))
    acc_scr[...] = acc
    run_scr[...] = run_scr[...] + jnp.sum(log_1mb, axis=1, keepdims=True)

    @pl.when(j == n_pages - 1)
    def _():
        o_ref[...] = acc


def sb_decode(q_heads, cache_k, cache_v, page_table, bias_col, layer):
    nb, n_pages = page_table.shape

    def page_spec():
        return pl.BlockSpec((None, None, PAGE_SIZE, HEADS, HEAD_DIM),
                            lambda b, j, pt: (layer, pt[b, n_pages - 1 - j], 0, 0, 0))

    return pl.pallas_call(
        functools.partial(_sb_decode_kernel, n_pages=n_pages),
        grid_spec=pltpu.PrefetchScalarGridSpec(
            num_scalar_prefetch=1,
            grid=(nb, n_pages),
            in_specs=[pl.BlockSpec((None, Q_ROWS, HEAD_DIM), lambda b, j, pt: (b, 0, 0)),
                      page_spec(), page_spec(),
                      pl.BlockSpec((None, Q_ROWS, 1), lambda b, j, pt: (layer, 0, 0))],
            out_specs=pl.BlockSpec((None, Q_ROWS, HEAD_DIM), lambda b, j, pt: (b, 0, 0)),
            scratch_shapes=[pltpu.VMEM((Q_ROWS, HEAD_DIM), F32), pltpu.VMEM((Q_ROWS, 1), F32)]),
        out_shape=jax.ShapeDtypeStruct((nb, Q_ROWS, HEAD_DIM), F32),
        compiler_params=_params("parallel", "arbitrary"),
        name="sb_decode",
    )(page_table, q_heads, cache_k, cache_v, bias_col)


def _rope_tables(pos):
    half = HEAD_DIM // 2
    inv = ROPE_THETA ** (-jnp.arange(half, dtype=F32) / half)
    ang = pos.astype(F32)[:, None] * inv[None, :]
    cos, sin = jnp.cos(ang), jnp.sin(ang)
    return jnp.concatenate([cos, cos], axis=-1), jnp.concatenate([-sin, sin], axis=-1)


def kernel(x_prompt, x_sample, cache_sb_k, cache_sb_v, state_ret, state_conv, page_table, p_prompt, p_sample,
           ffn1_norm, ffn1_w_gu, ffn1_w_down, mix_norm, w_in, ret_gn, conv_w, sb_q_norm, sb_k_norm, sb_bias,
           w_branch_ret, w_branch_conv, w_branch_sb, w_out, ffn2_norm, ffn2_w_gu, ffn2_w_down,
           ple_norm, w_ple_gate, w_ple_up):
    batch, seq, _ = x_prompt.shape
    nb = x_sample.shape[0]
    m_p = batch * seq
    m_s = 16

    bf = lambda w: w.astype(BF16)
    ffn1_w_gu, ffn1_w_down, w_in = bf(ffn1_w_gu), bf(ffn1_w_down), bf(w_in)
    w_branch_ret, w_branch_conv, w_branch_sb, w_out = bf(w_branch_ret), bf(w_branch_conv), bf(w_branch_sb), bf(w_out)
    ffn2_w_gu, ffn2_w_down, w_ple_gate, w_ple_up = bf(ffn2_w_gu), bf(ffn2_w_down), bf(w_ple_gate), bf(w_ple_up)

    lg = jnp.log1p(-jnp.exp2(-5.0 - jnp.arange(HEADS, dtype=F32)))
    cos_p, sin_p = _rope_tables(jnp.arange(seq, dtype=jnp.int32))
    past_len = page_table.shape[1] * PAGE_SIZE
    cos_s, sin_s = _rope_tables(jnp.full((1,), past_len, jnp.int32))
    tp = lambda t: min(t, m_p)
    bias_cols = jnp.pad(sb_bias, ((0, 0), (0, Q_ROWS - HEADS))).reshape(DEPTH, Q_ROWS, 1)

    xp = x_prompt.reshape(m_p, D_MODEL)
    xs = jnp.pad(x_sample.reshape(nb, D_MODEL), ((0, m_s - nb), (0, 0)))
    pp = p_prompt.reshape(DEPTH, m_p, P_DIM)
    ps = jnp.pad(p_sample.reshape(DEPTH, nb, P_DIM), ((0, 0), (0, m_s - nb), (0, 0)))

    def ffn(x, norm, w_gu, w_down, i, tm_norm, tm, tm_down):
        h = rmsnorm_bf16(x, norm, i, tm_norm)
        act = swiglu_mm(h, w_gu, i, tm, 512)
        return mm_residual(act, w_down, x, i, 0.5, tm_down, 512)

    def tail(x, proj, ya, yb, yc, p, i, tm_norm, tm, tm_merge):
        merged = merge_branches(ya, yb, yc, w_branch_ret, w_branch_conv, w_branch_sb, proj, i, tm_merge)
        x = mm_residual(merged, w_out, x, i, 1.0, tm, 1024)
        x = ffn(x, ffn2_norm, ffn2_w_gu, ffn2_w_down, i, tm_norm, tm, tm_merge)
        h = rmsnorm_bf16(x, ple_norm, i, tm_norm)
        return ple_update(h, w_ple_gate, p, w_ple_up, x, i, tm, 1024)

    outs = [[] for _ in range(8)]
    for i in range(DEPTH):
        xp = ffn(xp, ffn1_norm, ffn1_w_gu, ffn1_w_down, i, tp(256), tp(1024), tp(512))
        proj = mm_f32(rmsnorm_bf16(xp, mix_norm, i, tp(256)), w_in, i, tp(1024), 1024)
        ya, s_p = retention_prompt(proj, cos_p, sin_p, lg, ret_gn, i, batch, seq)
        yb, c_p = conv_prompt(proj, conv_w, i, batch, seq)
        sk_p = head_norm(proj, sb_k_norm, COL_SK, i, tp(512))
        yc = sb_prompt(proj, sk_p, sb_q_norm, sb_bias[i], i, batch, seq)
        sv_p = proj[:, COL_SV * MIX_W:(COL_SV + 1) * MIX_W]
        xp = tail(xp, proj, ya, yb, yc, pp[i], i, tp(256), tp(1024), tp(512))

        xs = ffn(xs, ffn1_norm, ffn1_w_gu, ffn1_w_down, i, m_s, m_s, m_s)
        proj_s = mm_f32(rmsnorm_bf16(xs, mix_norm, i, m_s), w_in, i, m_s, 1024)
        ya_s, yb_s, sq_s, sk_s, s_s, c_s = decode_mix(
            proj_s[:nb].reshape(nb, 1, N_IN), state_ret, state_conv, cos_s, sin_s, lg, ret_gn, conv_w,
            sb_q_norm, sb_k_norm, i, nb)
        q_heads = jnp.pad(sq_s.reshape(nb, HEADS, HEAD_DIM), ((0, 0), (0, Q_ROWS - HEADS), (0, 0)))
        yc_s = sb_decode(q_heads, cache_sb_k, cache_sb_v, page_table, bias_cols, i)
        yc_s = yc_s[:, :HEADS, :].reshape(nb, MIX_W).astype(BF16)
        pad_rows = lambda y: jnp.pad(y.reshape(nb, MIX_W), ((0, m_s - nb), (0, 0)))
        sv_s = proj_s[:nb, COL_SV * MIX_W:(COL_SV + 1) * MIX_W]
        xs = tail(xs, proj_s, pad_rows(ya_s), pad_rows(yb_s), pad_rows(yc_s), ps[i], i, m_s, m_s, m_s)

        for lst, val in zip(outs, (sk_p.reshape(batch, seq, HEADS, HEAD_DIM),
                                   sv_p.reshape(batch, seq, HEADS, HEAD_DIM),
                                   sk_s.reshape(nb, 1, HEADS, HEAD_DIM),
                                   sv_s.reshape(nb, 1, HEADS, HEAD_DIM),
                                   s_p, s_s, c_p, c_s)):
            lst.append(val)

    return (xp.reshape(batch, seq, D_MODEL), xs[:nb].reshape(nb, 1, D_MODEL)) + tuple(jnp.stack(o) for o in outs)
```

```python
import functools

import jax
import jax.numpy as jnp
from jax import lax
from jax.experimental import pallas as pl
from jax.experimental.pallas import tpu as pltpu

F32 = jnp.float32
BF16 = jnp.bfloat16

D_MODEL = 2048
DEPTH = 2
PAGE_SIZE = 128
HEADS = 8
HEAD_DIM = 128
MIX_W = HEADS * HEAD_DIM
CHUNK = 128
D_FF = 5632
P_DIM = 256
ROPE_THETA = 10000.0
EPS = 1e-6
N_IN = 9 * MIX_W + 3 * D_MODEL
COL_RQ, COL_RK, COL_RV, COL_CH, COL_CB, COL_CC, COL_SQ, COL_SK, COL_SV = range(9)
COL_GA, COL_GB, COL_GC = 9, 11, 13

VMEM_LIMIT_BYTES = 56 * 1024 * 1024


def _params(*sem):
    return pltpu.CompilerParams(dimension_semantics=sem, vmem_limit_bytes=VMEM_LIMIT_BYTES)


def _dot(a, b):
    return jnp.dot(a, b, preferred_element_type=F32)


def _dot_nt(a, b):
    return lax.dot_general(a, b, (((1,), (1,)), ((), ())), preferred_element_type=F32)


def _dot_tn(a, b):
    return lax.dot_general(a, b, (((0,), (0,)), ((), ())), preferred_element_type=F32)


def _rms(x, g):
    return x * lax.rsqrt(jnp.mean(x * x, axis=-1, keepdims=True) + EPS) * g


def _sigmoid(x):
    return jax.nn.sigmoid(x)


def _rmsnorm_kernel(x_ref, g_ref, o_ref):
    o_ref[...] = _rms(x_ref[...], g_ref[...]).astype(o_ref.dtype)


def rmsnorm_bf16(x, g, layer, tm):
    m, d = x.shape
    return pl.pallas_call(
        _rmsnorm_kernel,
        grid=(m // tm,),
        in_specs=[pl.BlockSpec((tm, d), lambda i: (i, 0)),
                  pl.BlockSpec((None, 1, d), lambda i: (layer, 0, 0))],
        out_specs=pl.BlockSpec((tm, d), lambda i: (i, 0)),
        out_shape=jax.ShapeDtypeStruct((m, d), BF16),
        compiler_params=_params("parallel"),
        name="rmsnorm",
    )(x, g.reshape(DEPTH, 1, d))


def _swiglu_kernel(a_ref, wg_ref, wu_ref, o_ref):
    a = a_ref[...]
    g = _dot(a, wg_ref[...])
    u = _dot(a, wu_ref[...])
    o_ref[...] = (g * _sigmoid(g) * u).astype(o_ref.dtype)


def swiglu_mm(a, w_gu, layer, tm, tn):
    m, k = a.shape
    nf = D_FF // tn
    return pl.pallas_call(
        _swiglu_kernel,
        grid=(nf, m // tm),
        in_specs=[pl.BlockSpec((tm, k), lambda j, i: (i, 0)),
                  pl.BlockSpec((None, k, tn), lambda j, i: (layer, 0, j)),
                  pl.BlockSpec((None, k, tn), lambda j, i: (layer, 0, j + nf))],
        out_specs=pl.BlockSpec((tm, tn), lambda j, i: (i, j)),
        out_shape=jax.ShapeDtypeStruct((m, D_FF), BF16),
        compiler_params=_params("parallel", "parallel"),
        name="swiglu_mm",
    )(a, w_gu, w_gu)


def _mm_res_kernel(a_ref, w_ref, r_ref, o_ref, *, scale):
    o_ref[...] = r_ref[...] + scale * _dot(a_ref[...], w_ref[...])


def mm_residual(a, w, res, layer, scale, tm, tn):
    m, k = a.shape
    n = w.shape[-1]
    return pl.pallas_call(
        functools.partial(_mm_res_kernel, scale=scale),
        grid=(n // tn, m // tm),
        in_specs=[pl.BlockSpec((tm, k), lambda j, i: (i, 0)),
                  pl.BlockSpec((None, k, tn), lambda j, i: (layer, 0, j)),
                  pl.BlockSpec((tm, tn), lambda j, i: (i, j))],
        out_specs=pl.BlockSpec((tm, tn), lambda j, i: (i, j)),
        out_shape=jax.ShapeDtypeStruct((m, n), F32),
        compiler_params=_params("parallel", "parallel"),
        name="mm_residual",
    )(a, w, res)


def _mm_kernel(a_ref, w_ref, o_ref):
    o_ref[...] = _dot(a_ref[...], w_ref[...])


def mm_f32(a, w, layer, tm, tn):
    m, k = a.shape
    n = w.shape[-1]
    return pl.pallas_call(
        _mm_kernel,
        grid=(n // tn, m // tm),
        in_specs=[pl.BlockSpec((tm, k), lambda j, i: (i, 0)),
                  pl.BlockSpec((None, k, tn), lambda j, i: (layer, 0, j))],
        out_specs=pl.BlockSpec((tm, tn), lambda j, i: (i, j)),
        out_shape=jax.ShapeDtypeStruct((m, n), F32),
        compiler_params=_params("parallel", "parallel"),
        name="mm_in_proj",
    )(a, w)


def _merge_kernel(ya_ref, yb_ref, yc_ref, wr_ref, wc_ref, ws_ref, ga_ref, gb_ref, gc_ref, o_ref):
    m = (_sigmoid(ga_ref[...]) * _dot(ya_ref[...], wr_ref[...])
         + _sigmoid(gb_ref[...]) * _dot(yb_ref[...], wc_ref[...])
         + _sigmoid(gc_ref[...]) * _dot(yc_ref[...], ws_ref[...]))
    o_ref[...] = m.astype(o_ref.dtype)


def merge_branches(ya, yb, yc, w_ret, w_conv, w_sb, proj, layer, tm):
    m = ya.shape[0]
    tn = MIX_W
    y_spec = pl.BlockSpec((tm, MIX_W), lambda j, i: (i, 0))
    w_spec = pl.BlockSpec((None, MIX_W, tn), lambda j, i: (layer, 0, j))

    def gate_spec(col):
        return pl.BlockSpec((tm, tn), lambda j, i: (i, col + j))

    return pl.pallas_call(
        _merge_kernel,
        grid=(D_MODEL // tn, m // tm),
        in_specs=[y_spec, y_spec, y_spec, w_spec, w_spec, w_spec,
                  gate_spec(COL_GA), gate_spec(COL_GB), gate_spec(COL_GC)],
        out_specs=pl.BlockSpec((tm, tn), lambda j, i: (i, j)),
        out_shape=jax.ShapeDtypeStruct((m, D_MODEL), BF16),
        compiler_params=_params("parallel", "parallel"),
        name="merge_branches",
    )(ya, yb, yc, w_ret, w_conv, w_sb, proj, proj, proj)


def _ple_kernel(h_ref, wg_ref, p_ref, wu_ref, x_ref, o_ref):
    gate = _dot(h_ref[...], wg_ref[...])
    up = _dot(p_ref[...].astype(BF16), wu_ref[...])
    o_ref[...] = x_ref[...] + _sigmoid(gate) * up


def ple_update(h, w_gate, p, w_up, x, layer, tm, tn):
    m = h.shape[0]
    return pl.pallas_call(
        _ple_kernel,
        grid=(D_MODEL // tn, m // tm),
        in_specs=[pl.BlockSpec((tm, D_MODEL), lambda j, i: (i, 0)),
                  pl.BlockSpec((None, D_MODEL, tn), lambda j, i: (layer, 0, j)),
                  pl.BlockSpec((tm, P_DIM), lambda j, i: (i, 0)),
                  pl.BlockSpec((None, P_DIM, tn), lambda j, i: (layer, 0, j)),
                  pl.BlockSpec((tm, tn), lambda j, i: (i, j))],
        out_specs=pl.BlockSpec((tm, tn), lambda j, i: (i, j)),
        out_shape=jax.ShapeDtypeStruct((m, D_MODEL), F32),
        compiler_params=_params("parallel", "parallel"),
        name="ple_update",
    )(h, w_gate, p, w_up, x)


def _rope(x, cos2, sin2):
    return x * cos2 + pltpu.roll(x, HEAD_DIM // 2, 1) * sin2


def _softplus_neg_abs(z):
    return jnp.log(1.0 + jnp.exp(-jnp.abs(z)))


def _split_bf16(x):
    hi = x.astype(BF16)
    return hi, (x - hi.astype(F32)).astype(BF16)


def _retention_kernel(lg_ref, q_ref, k_ref, v_ref, cos_ref, sin_ref, gn_ref, ya_ref, s_out_ref, s_scr,
                      *, n_chunks):
    c = pl.program_id(1)

    @pl.when(c == 0)
    def _():
        s_scr[...] = jnp.zeros_like(s_scr)

    cos2 = cos_ref[...]
    sin2 = sin_ref[...]
    ii = lax.broadcasted_iota(jnp.int32, (CHUNK, CHUNK), 0)
    jj = lax.broadcasted_iota(jnp.int32, (CHUNK, CHUNK), 1)
    diff = (ii - jj).astype(F32)
    pos = lax.broadcasted_iota(jnp.int32, (CHUNK, 1), 0).astype(F32)
    for h in range(HEADS):
        cols = slice(h * HEAD_DIM, (h + 1) * HEAD_DIM)
        lg = lg_ref[h]
        q = _rope(q_ref[:, cols], cos2, sin2)
        k = _rope(k_ref[:, cols], cos2, sin2) * (HEAD_DIM ** -0.5)
        v = v_ref[:, cols].astype(BF16)
        qb = q.astype(BF16)
        dmask = jnp.where(diff >= 0, jnp.exp(lg * jnp.maximum(diff, 0.0)), 0.0)
        scores = _dot_nt(qb, k.astype(BF16)) * dmask
        inner = _dot(scores.astype(BF16), v)
        s = s_scr[h]
        cross = _dot(qb, s.astype(BF16)) * jnp.exp(lg * (pos + 1.0))
        kd = (k * jnp.exp(lg * (CHUNK - 1.0 - pos))).astype(BF16)
        s_scr[h] = jnp.exp(lg * CHUNK) * s + _dot_tn(kd, v)
        o = inner + cross
        ya_ref[:, cols] = _rms(o, gn_ref[h:h + 1, :]).astype(ya_ref.dtype)

    @pl.when(c == n_chunks - 1)
    def _():
        s_out_ref[...] = s_scr[...]


def retention_prompt(proj, cos2, sin2, lg, gn, layer, batch, seq):
    n_chunks = seq // CHUNK
    m = batch * seq

    def col_spec(col):
        return pl.BlockSpec((CHUNK, MIX_W), lambda b, c, lg_ref: (b * n_chunks + c, col))

    tab_spec = pl.BlockSpec((CHUNK, HEAD_DIM), lambda b, c, lg_ref: (c, 0))
    return pl.pallas_call(
        functools.partial(_retention_kernel, n_chunks=n_chunks),
        grid_spec=pltpu.PrefetchScalarGridSpec(
            num_scalar_prefetch=1,
            grid=(batch, n_chunks),
            in_specs=[col_spec(COL_RQ), col_spec(COL_RK), col_spec(COL_RV), tab_spec, tab_spec,
                      pl.BlockSpec((None, HEADS, HEAD_DIM), lambda b, c, lg_ref: (layer, 0, 0))],
            out_specs=[pl.BlockSpec((CHUNK, MIX_W), lambda b, c, lg_ref: (b * n_chunks + c, 0)),
                       pl.BlockSpec((None, HEADS, HEAD_DIM, HEAD_DIM), lambda b, c, lg_ref: (b, 0, 0, 0))],
            scratch_shapes=[pltpu.VMEM((HEADS, HEAD_DIM, HEAD_DIM), F32)]),
        out_shape=[jax.ShapeDtypeStruct((m, MIX_W), BF16),
                   jax.ShapeDtypeStruct((batch, HEADS, HEAD_DIM, HEAD_DIM), F32)],
        compiler_params=_params("parallel", "arbitrary"),
        name="retention_prompt",
    )(lg, proj, proj, proj, cos2, sin2, gn)


def _conv_kernel(ch_ref, cb_ref, cc_ref, w_ref, yb_ref, cn_ref):
    u = cc_ref[...] * ch_ref[...]
    seq = u.shape[0]
    t = lax.broadcasted_iota(jnp.int32, u.shape, 0)
    u1 = jnp.where(t >= 1, pltpu.roll(u, 1, 0), 0.0)
    u2 = jnp.where(t >= 2, pltpu.roll(u, 2, 0), 0.0)
    cv = u2 * w_ref[0:1, :] + u1 * w_ref[1:2, :] + u * w_ref[2:3, :]
    yb_ref[...] = (cb_ref[...] * cv).astype(yb_ref.dtype)
    cn_ref[...] = cc_ref[seq - 2:seq, :] * ch_ref[seq - 2:seq, :]


def conv_prompt(proj, conv_w, layer, batch, seq):
    tw = 256
    nw = MIX_W // tw

    def col_spec(col):
        return pl.BlockSpec((seq, tw), lambda b, j: (b, col * nw + j))

    return pl.pallas_call(
        _conv_kernel,
        grid=(batch, nw),
        in_specs=[col_spec(COL_CH), col_spec(COL_CB), col_spec(COL_CC),
                  pl.BlockSpec((None, 3, tw), lambda b, j: (layer, 0, j))],
        out_specs=[pl.BlockSpec((seq, tw), lambda b, j: (b, j)),
                   pl.BlockSpec((None, 2, tw), lambda b, j: (b, 0, j))],
        out_shape=[jax.ShapeDtypeStruct((batch * seq, MIX_W), BF16),
                   jax.ShapeDtypeStruct((batch, 2, MIX_W), F32)],
        compiler_params=_params("parallel", "parallel"),
        name="conv_prompt",
    )(proj, proj, proj, conv_w)


def _head_norm_kernel(x_ref, g_ref, o_ref):
    g = g_ref[...]
    for h in range(HEADS):
        cols = slice(h * HEAD_DIM, (h + 1) * HEAD_DIM)
        o_ref[:, cols] = _rms(x_ref[:, cols], g)


def head_norm(proj, g, col, layer, tm):
    m = proj.shape[0]
    return pl.pallas_call(
        _head_norm_kernel,
        grid=(m // tm,),
        in_specs=[pl.BlockSpec((tm, MIX_W), lambda i: (i, col)),
                  pl.BlockSpec((None, 1, HEAD_DIM), lambda i: (layer, 0, 0))],
        out_specs=pl.BlockSpec((tm, MIX_W), lambda i: (i, 0)),
        out_shape=jax.ShapeDtypeStruct((m, MIX_W), F32),
        compiler_params=_params("parallel"),
        name="head_norm",
    )(proj, g.reshape(DEPTH, 1, HEAD_DIM))


SB_TILE = 256


SB_HEADS_PER_STEP = 4


def _sb_prompt_kernel(bias_ref, q_ref, k_ref, v_ref, qg_ref, o_ref, acc_scr, run_scr):
    t = SB_TILE
    hg = pl.program_id(1)
    qi = pl.program_id(2)
    row = lax.broadcasted_iota(jnp.int32, (t, t), 0)
    col = lax.broadcasted_iota(jnp.int32, (t, t), 1)
    causal = col < row
    r2 = lax.broadcasted_iota(jnp.int32, (t, t + HEAD_DIM), 0)
    c2 = lax.broadcasted_iota(jnp.int32, (t, t + HEAD_DIM), 1)
    scan = jnp.where((r2 > c2) | (c2 >= t), 1.0, 0.0).astype(BF16)
    acc_scr[...] = jnp.zeros_like(acc_scr)
    run_scr[...] = jnp.zeros_like(run_scr)
    heads = range(SB_HEADS_PER_STEP)
    cols = [slice(hh * HEAD_DIM, (hh + 1) * HEAD_DIM) for hh in heads]
    qn = [_rms(q_ref[:, cols[hh]], qg_ref[...]).astype(BF16) for hh in heads]
    bias = [bias_ref[hg * SB_HEADS_PER_STEP + hh] for hh in heads]

    def block(kj, diagonal):
        start = pl.multiple_of(kj * t, t)
        for hh in heads:
            kb = k_ref[pl.ds(start, t), cols[hh]].astype(BF16)
            vb = v_ref[pl.ds(start, t), cols[hh]].astype(BF16)
            z = _dot_nt(qn[hh], kb) * (HEAD_DIM ** -0.5) + bias[hh]
            sp = _softplus_neg_abs(z)
            log_b = jnp.minimum(z, 0.0) - sp
            log_1mb = -jnp.maximum(z, 0.0) - sp
            if diagonal:
                log_1mb = jnp.where(causal, log_1mb, 0.0)
            hi, lo = _split_bf16(log_1mb)
            r = _dot(hi, scan) + _dot(lo, scan)
            run = run_scr[hh]
            e = jnp.exp(log_b + r[:, :t] + jnp.concatenate([run] * (t // HEAD_DIM), axis=1))
            if diagonal:
                e = jnp.where(causal, e, 0.0)
            acc_scr[hh] += _dot(e.astype(BF16), vb)
            run_scr[hh] = run + r[:, t:]

    block(qi, True)

    def body(i, carry):
        block(qi - 1 - i, False)
        return carry

    lax.fori_loop(0, qi, body, 0)
    for hh in heads:
        o_ref[:, cols[hh]] = acc_scr[hh].astype(o_ref.dtype)


def sb_prompt(proj, sk, q_gain, bias, layer, batch, seq):
    t = SB_TILE
    nq = seq // t
    hs = SB_HEADS_PER_STEP
    width = hs * HEAD_DIM
    groups = HEADS // hs
    return pl.pallas_call(
        _sb_prompt_kernel,
        grid_spec=pltpu.PrefetchScalarGridSpec(
            num_scalar_prefetch=1,
            grid=(batch, groups, nq),
            in_specs=[pl.BlockSpec((t, width), lambda b, g, qi, s: (b * nq + qi, COL_SQ * groups + g)),
                      pl.BlockSpec((seq, width), lambda b, g, qi, s: (b, g)),
                      pl.BlockSpec((seq, width), lambda b, g, qi, s: (b, COL_SV * groups + g)),
                      pl.BlockSpec((None, 1, HEAD_DIM), lambda b, g, qi, s: (layer, 0, 0))],
            out_specs=pl.BlockSpec((t, width), lambda b, g, qi, s: (b * nq + qi, g)),
            scratch_shapes=[pltpu.VMEM((hs, t, HEAD_DIM), F32), pltpu.VMEM((hs, t, HEAD_DIM), F32)]),
        out_shape=jax.ShapeDtypeStruct((batch * seq, MIX_W), BF16),
        compiler_params=_params("parallel", "parallel", "parallel"),
        name="sb_prompt",
    )(bias, proj, sk, proj, q_gain.reshape(DEPTH, 1, HEAD_DIM))


def _decode_mix_kernel(lg_ref, p_ref, s_ref, cprev_ref, cos_ref, sin_ref, gn_ref, cw_ref, qg_ref, kg_ref,
                       ya_ref, yb_ref, sq_ref, sk_ref, s_out_ref, cn_ref):
    cos2 = cos_ref[...]
    sin2 = sin_ref[...]
    row = lax.broadcasted_iota(jnp.int32, (HEAD_DIM, HEAD_DIM), 0)
    col = lax.broadcasted_iota(jnp.int32, (HEAD_DIM, HEAD_DIM), 1)

    def stream(c, h):
        return p_ref[:, c * MIX_W + h * HEAD_DIM:c * MIX_W + (h + 1) * HEAD_DIM]

    def bf_round(x):
        return x.astype(BF16).astype(F32)

    for h in range(HEADS):
        cols = slice(h * HEAD_DIM, (h + 1) * HEAD_DIM)
        lg = lg_ref[h]
        q = bf_round(_rope(stream(COL_RQ, h), cos2, sin2))
        k = bf_round(_rope(stream(COL_RK, h), cos2, sin2) * (HEAD_DIM ** -0.5))
        v = bf_round(stream(COL_RV, h))
        s = s_ref[h]
        score = bf_round(jnp.sum(q * k, axis=-1, keepdims=True))
        inner = score * v
        q_rows = jnp.broadcast_to(q, (16, HEAD_DIM)).astype(BF16)
        cross = _dot(q_rows, s.astype(BF16))[0:1, :] * jnp.exp(lg * jnp.ones((1, 1), F32))
        k_col = jnp.sum(jnp.where(row == col, jnp.broadcast_to(k, (HEAD_DIM, HEAD_DIM)), 0.0),
                        axis=1, keepdims=True)
        s_out_ref[h] = jnp.exp(lg * jnp.ones((1, 1), F32)) * s + k_col * v
        ya_ref[:, cols] = _rms(inner + cross, gn_ref[h:h + 1, :]).astype(ya_ref.dtype)
        sq_ref[:, cols] = _rms(stream(COL_SQ, h), qg_ref[...])
        sk_ref[:, cols] = _rms(stream(COL_SK, h), kg_ref[...])

    ch = p_ref[:, COL_CH * MIX_W:(COL_CH + 1) * MIX_W]
    cb = p_ref[:, COL_CB * MIX_W:(COL_CB + 1) * MIX_W]
    cc = p_ref[:, COL_CC * MIX_W:(COL_CC + 1) * MIX_W]
    u = cc * ch
    cv = cprev_ref[0:1, :] * cw_ref[0:1, :] + cprev_ref[1:2, :] * cw_ref[1:2, :] + u * cw_ref[2:3, :]
    yb_ref[...] = (cb * cv).astype(yb_ref.dtype)
    cn_ref[0:1, :] = cprev_ref[1:2, :]
    cn_ref[1:2, :] = u


def decode_mix(proj_s, state_ret, state_conv, cos2, sin2, lg, gn, conv_w, q_gain, k_gain, layer, nb):
    def row_spec(width):
        return pl.BlockSpec((None, 1, width), lambda b, s: (b, 0, 0))

    def w_spec(*shape):
        return pl.BlockSpec((None,) + shape, lambda b, s: (layer,) + (0,) * len(shape))

    tab_spec = pl.BlockSpec((1, HEAD_DIM), lambda b, s: (0, 0))
    return pl.pallas_call(
        _decode_mix_kernel,
        grid_spec=pltpu.PrefetchScalarGridSpec(
            num_scalar_prefetch=1,
            grid=(nb,),
            in_specs=[row_spec(N_IN),
                      pl.BlockSpec((None, None, HEADS, HEAD_DIM, HEAD_DIM), lambda b, s: (layer, b, 0, 0, 0)),
                      pl.BlockSpec((None, None, 2, MIX_W), lambda b, s: (layer, b, 0, 0)),
                      tab_spec, tab_spec, w_spec(HEADS, HEAD_DIM), w_spec(3, MIX_W),
                      w_spec(1, HEAD_DIM), w_spec(1, HEAD_DIM)],
            out_specs=[row_spec(MIX_W), row_spec(MIX_W), row_spec(MIX_W), row_spec(MIX_W),
                       pl.BlockSpec((None, HEADS, HEAD_DIM, HEAD_DIM), lambda b, s: (b, 0, 0, 0)),
                       pl.BlockSpec((None, 2, MIX_W), lambda b, s: (b, 0, 0))]),
        out_shape=[jax.ShapeDtypeStruct((nb, 1, MIX_W), BF16),
                   jax.ShapeDtypeStruct((nb, 1, MIX_W), BF16),
                   jax.ShapeDtypeStruct((nb, 1, MIX_W), F32),
                   jax.ShapeDtypeStruct((nb, 1, MIX_W), F32),
                   jax.ShapeDtypeStruct((nb, HEADS, HEAD_DIM, HEAD_DIM), F32),
                   jax.ShapeDtypeStruct((nb, 2, MIX_W), F32)],
        compiler_params=_params("parallel"),
        name="decode_mix",
    )(lg, proj_s, state_ret, state_conv, cos2, sin2, gn, conv_w,
      q_gain.reshape(DEPTH, 1, HEAD_DIM), k_gain.reshape(DEPTH, 1, HEAD_DIM))


PAGES_PER_STEP = 4
WINDOW = PAGES_PER_STEP * PAGE_SIZE
LANES = 128
SUB_ROWS = 16


def _sb_decode_kernel(pt_ref, q_ref, bias_ref, *refs, n_steps):
    g_n = PAGES_PER_STEP
    k_refs, v_refs = refs[:g_n], refs[g_n:2 * g_n]
    o_ref, qb_scr, scan_scr, acc_scr, run_scr = refs[2 * g_n:]
    s = pl.program_id(1)
    head_row = lax.broadcasted_iota(jnp.int32, (SUB_ROWS, WINDOW), 0)

    def head_window(page_refs, h):
        return jnp.concatenate(
            [r[pl.ds(h, PAGE_SIZE, stride=HEADS), :].astype(BF16) for r in page_refs], axis=0)

    @pl.when(s == 0)
    def _():
        acc_scr[...] = jnp.zeros_like(acc_scr)
        run_scr[...] = jnp.zeros_like(run_scr)
        q = q_ref[...]
        for h in range(HEADS):
            qb_scr[:, h * HEAD_DIM:(h + 1) * HEAD_DIM] = jnp.where(head_row[:, :HEAD_DIM] == h, q, 0.0).astype(BF16)
        r2 = lax.broadcasted_iota(jnp.int32, (WINDOW, WINDOW + LANES), 0)
        c2 = lax.broadcasted_iota(jnp.int32, (WINDOW, WINDOW + LANES), 1)
        scan_scr[...] = jnp.where((r2 > c2) | (c2 >= WINDOW), 1.0, 0.0).astype(BF16)

    k_all = jnp.concatenate([head_window(k_refs, h) for h in range(HEADS)], axis=1)
    z = _dot_nt(qb_scr[...], k_all) * (HEAD_DIM ** -0.5) + bias_ref[...]
    sp = _softplus_neg_abs(z)
    log_b = jnp.minimum(z, 0.0) - sp
    log_1mb = -jnp.maximum(z, 0.0) - sp
    hi, lo = _split_bf16(log_1mb)
    r = _dot(jnp.concatenate([hi, lo], axis=0), scan_scr[...])
    r = r[:SUB_ROWS, :] + r[SUB_ROWS:, :]
    run = run_scr[...]
    a = jnp.exp(log_b + r[:, :WINDOW] + jnp.concatenate([run] * (WINDOW // LANES), axis=1))
    run_scr[...] = run + r[:, WINDOW:]

    a_all = jnp.concatenate([jnp.where(head_row == h, a, 0.0).astype(BF16) for h in range(HEADS)], axis=1)
    v_all = jnp.concatenate([head_window(v_refs, h) for h in range(HEADS)], axis=0)
    acc = acc_scr[...] + _dot(a_all, v_all)
    acc_scr[...] = acc

    @pl.when(s == n_steps - 1)
    def _():
        o_ref[...] = acc[0:HEADS, :]


def sb_decode(q_rows, bias_col, cache_k, cache_v, page_table, layer):
    nb, n_pages = page_table.shape
    n_steps = n_pages // PAGES_PER_STEP

    def page_spec(g):
        return pl.BlockSpec(
            (None, None, PAGE_SIZE * HEADS, HEAD_DIM),
            lambda b, s, pt: (layer, pt[b, n_pages - (s + 1) * PAGES_PER_STEP + g], 0, 0))

    pages = [page_spec(g) for g in range(PAGES_PER_STEP)]
    return pl.pallas_call(
        functools.partial(_sb_decode_kernel, n_steps=n_steps),
        grid_spec=pltpu.PrefetchScalarGridSpec(
            num_scalar_prefetch=1,
            grid=(nb, n_steps),
            in_specs=[pl.BlockSpec((None, SUB_ROWS, HEAD_DIM), lambda b, s, pt: (b, 0, 0)),
                      pl.BlockSpec((None, SUB_ROWS, 1), lambda b, s, pt: (layer, 0, 0))] + pages + pages,
            out_specs=pl.BlockSpec((None, HEADS, HEAD_DIM), lambda b, s, pt: (b, 0, 0)),
            scratch_shapes=[pltpu.VMEM((SUB_ROWS, HEADS * HEAD_DIM), BF16),
                            pltpu.VMEM((WINDOW, WINDOW + LANES), BF16),
                            pltpu.VMEM((SUB_ROWS, HEAD_DIM), F32),
                            pltpu.VMEM((SUB_ROWS, LANES), F32)]),
        out_shape=jax.ShapeDtypeStruct((nb, HEADS, HEAD_DIM), F32),
        compiler_params=_params("parallel", "arbitrary"),
        name="sb_decode",
    )(page_table, q_rows, bias_col, *([cache_k] * PAGES_PER_STEP), *([cache_v] * PAGES_PER_STEP))


def _rope_tables(pos):
    half = HEAD_DIM // 2
    inv = ROPE_THETA ** (-jnp.arange(half, dtype=F32) / half)
    ang = pos.astype(F32)[:, None] * inv[None, :]
    cos, sin = jnp.cos(ang), jnp.sin(ang)
    return jnp.concatenate([cos, cos], axis=-1), jnp.concatenate([-sin, sin], axis=-1)


def kernel(x_prompt, x_sample, cache_sb_k, cache_sb_v, state_ret, state_conv, page_table, p_prompt, p_sample,
           ffn1_norm, ffn1_w_gu, ffn1_w_down, mix_norm, w_in, ret_gn, conv_w, sb_q_norm, sb_k_norm, sb_bias,
           w_branch_ret, w_branch_conv, w_branch_sb, w_out, ffn2_norm, ffn2_w_gu, ffn2_w_down,
           ple_norm, w_ple_gate, w_ple_up):
    batch, seq, _ = x_prompt.shape
    nb = x_sample.shape[0]
    m_p = batch * seq
    m_s = 16

    bf = lambda w: w.astype(BF16)
    ffn1_w_gu, ffn1_w_down, w_in = bf(ffn1_w_gu), bf(ffn1_w_down), bf(w_in)
    w_branch_ret, w_branch_conv, w_branch_sb, w_out = bf(w_branch_ret), bf(w_branch_conv), bf(w_branch_sb), bf(w_out)
    ffn2_w_gu, ffn2_w_down, w_ple_gate, w_ple_up = bf(ffn2_w_gu), bf(ffn2_w_down), bf(w_ple_gate), bf(w_ple_up)

    lg = jnp.log1p(-jnp.exp2(-5.0 - jnp.arange(HEADS, dtype=F32)))
    cos_p, sin_p = _rope_tables(jnp.arange(seq, dtype=jnp.int32))
    past_len = page_table.shape[1] * PAGE_SIZE
    cos_s, sin_s = _rope_tables(jnp.full((1,), past_len, jnp.int32))
    tp = lambda t: min(t, m_p)
    page_rows = cache_sb_k.shape[:2] + (PAGE_SIZE * HEADS, HEAD_DIM)
    cache_sb_k, cache_sb_v = cache_sb_k.reshape(page_rows), cache_sb_v.reshape(page_rows)
    row_pad = ((0, 0), (0, SUB_ROWS - HEADS), (0, 0))
    bias_cols = jnp.pad(sb_bias.reshape(DEPTH, HEADS, 1), row_pad)

    xp = x_prompt.reshape(m_p, D_MODEL)
    xs = jnp.pad(x_sample.reshape(nb, D_MODEL), ((0, m_s - nb), (0, 0)))
    pp = p_prompt.reshape(DEPTH, m_p, P_DIM)
    ps = jnp.pad(p_sample.reshape(DEPTH, nb, P_DIM), ((0, 0), (0, m_s - nb), (0, 0)))

    def ffn(x, norm, w_gu, w_down, i, tm_norm, tm, tm_down):
        h = rmsnorm_bf16(x, norm, i, tm_norm)
        act = swiglu_mm(h, w_gu, i, tm, 512)
        return mm_residual(act, w_down, x, i, 0.5, tm_down, 512)

    def tail(x, proj, ya, yb, yc, p, i, tm_norm, tm, tm_merge):
        merged = merge_branches(ya, yb, yc, w_branch_ret, w_branch_conv, w_branch_sb, proj, i, tm_merge)
        x = mm_residual(merged, w_out, x, i, 1.0, tm, 1024)
        x = ffn(x, ffn2_norm, ffn2_w_gu, ffn2_w_down, i, tm_norm, tm, tm_merge)
        h = rmsnorm_bf16(x, ple_norm, i, tm_norm)
        return ple_update(h, w_ple_gate, p, w_ple_up, x, i, tm, 1024)

    outs = [[] for _ in range(8)]
    for i in range(DEPTH):
        xp = ffn(xp, ffn1_norm, ffn1_w_gu, ffn1_w_down, i, tp(256), tp(1024), tp(512))
        proj = mm_f32(rmsnorm_bf16(xp, mix_norm, i, tp(256)), w_in, i, tp(1024), 1024)
        ya, s_p = retention_prompt(proj, cos_p, sin_p, lg, ret_gn, i, batch, seq)
        yb, c_p = conv_prompt(proj, conv_w, i, batch, seq)
        sk_p = head_norm(proj, sb_k_norm, COL_SK, i, tp(512))
        yc = sb_prompt(proj, sk_p, sb_q_norm, sb_bias[i], i, batch, seq)
        sv_p = proj[:, COL_SV * MIX_W:(COL_SV + 1) * MIX_W]
        xp = tail(xp, proj, ya, yb, yc, pp[i], i, tp(256), tp(1024), tp(512))

        xs = ffn(xs, ffn1_norm, ffn1_w_gu, ffn1_w_down, i, m_s, m_s, m_s)
        proj_s = mm_f32(rmsnorm_bf16(xs, mix_norm, i, m_s), w_in, i, m_s, 1024)
        ya_s, yb_s, sq_s, sk_s, s_s, c_s = decode_mix(
            proj_s[:nb].reshape(nb, 1, N_IN), state_ret, state_conv, cos_s, sin_s, lg, ret_gn, conv_w,
            sb_q_norm, sb_k_norm, i, nb)
        q_rows = jnp.pad(sq_s.reshape(nb, HEADS, HEAD_DIM), row_pad)
        yc_s = sb_decode(q_rows, bias_cols, cache_sb_k, cache_sb_v, page_table, i)
        yc_s = yc_s.reshape(nb, MIX_W).astype(BF16)
        pad_rows = lambda y: jnp.pad(y.reshape(nb, MIX_W), ((0, m_s - nb), (0, 0)))
        sv_s = proj_s[:nb, COL_SV * MIX_W:(COL_SV + 1) * MIX_W]
        xs = tail(xs, proj_s, pad_rows(ya_s), pad_rows(yb_s), pad_rows(yc_s), ps[i], i, m_s, m_s, m_s)

        for lst, val in zip(outs, (sk_p.reshape(batch, seq, HEADS, HEAD_DIM),
                                   sv_p.reshape(batch, seq, HEADS, HEAD_DIM),
                                   sk_s.reshape(nb, 1, HEADS, HEAD_DIM),
                                   sv_s.reshape(nb, 1, HEADS, HEAD_DIM),
                                   s_p, s_s, c_p, c_s)):
            lst.append(val)

    return (xp.reshape(batch, seq, D_MODEL), xs[:nb].reshape(nb, 1, D_MODEL)) + tuple(jnp.stack(o) for o in outs)
```

```python
import functools

import jax
import jax.numpy as jnp
from jax import lax
from jax.experimental import pallas as pl
from jax.experimental.pallas import tpu as pltpu

F32 = jnp.float32
BF16 = jnp.bfloat16

D_MODEL = 2048
DEPTH = 2
PAGE_SIZE = 128
HEADS = 8
HEAD_DIM = 128
MIX_W = HEADS * HEAD_DIM
CHUNK = 128
D_FF = 5632
P_DIM = 256
ROPE_THETA = 10000.0
EPS = 1e-6
N_IN = 9 * MIX_W + 3 * D_MODEL
COL_RQ, COL_RK, COL_RV, COL_CH, COL_CB, COL_CC, COL_SQ, COL_SK, COL_SV = range(9)
COL_GA, COL_GB, COL_GC = 9, 11, 13

VMEM_LIMIT_BYTES = 56 * 1024 * 1024


def _params(*sem):
    return pltpu.CompilerParams(dimension_semantics=sem, vmem_limit_bytes=VMEM_LIMIT_BYTES)


def _dot(a, b):
    return jnp.dot(a, b, preferred_element_type=F32)


def _dot_nt(a, b):
    return lax.dot_general(a, b, (((1,), (1,)), ((), ())), preferred_element_type=F32)


def _dot_tn(a, b):
    return lax.dot_general(a, b, (((0,), (0,)), ((), ())), preferred_element_type=F32)


def _rms(x, g):
    return x * lax.rsqrt(jnp.mean(x * x, axis=-1, keepdims=True) + EPS) * g


def _sigmoid(x):
    return jax.nn.sigmoid(x)


def _rmsnorm_kernel(xp_ref, xs_ref, g_ref, op_ref, os_ref):
    op_ref[...] = _rms(xp_ref[...], g_ref[...]).astype(op_ref.dtype)

    @pl.when(pl.program_id(0) == 0)
    def _():
        os_ref[...] = _rms(xs_ref[...], g_ref[...]).astype(os_ref.dtype)


def rmsnorm_bf16(xp, xs, g, layer, tm):
    m, d = xp.shape
    ms = xs.shape[0]
    return pl.pallas_call(
        _rmsnorm_kernel,
        grid=(m // tm,),
        in_specs=[pl.BlockSpec((tm, d), lambda i: (i, 0)),
                  pl.BlockSpec((ms, d), lambda i: (0, 0)),
                  pl.BlockSpec((None, 1, d), lambda i: (layer, 0, 0))],
        out_specs=[pl.BlockSpec((tm, d), lambda i: (i, 0)),
                   pl.BlockSpec((ms, d), lambda i: (0, 0))],
        out_shape=[jax.ShapeDtypeStruct((m, d), BF16), jax.ShapeDtypeStruct((ms, d), BF16)],
        compiler_params=_params("arbitrary"),
        name="rmsnorm",
    )(xp, xs, g.reshape(DEPTH, 1, d))


def _dense_kernel(*refs, body, n_rows, n_w):
    rows_p, rows_s = refs[:n_rows], refs[n_rows:2 * n_rows]
    w_refs = refs[2 * n_rows:2 * n_rows + n_w]
    op_ref, os_ref = refs[2 * n_rows + n_w:2 * n_rows + n_w + 2]
    w_scr = refs[2 * n_rows + n_w + 2:]

    @pl.when(pl.program_id(1) == 0)
    def _():
        for w_ref, scr in zip(w_refs, w_scr):
            scr[...] = w_ref[...].astype(BF16)
        os_ref[...] = body([r[...] for r in rows_s], [scr[...] for scr in w_scr]).astype(os_ref.dtype)

    op_ref[...] = body([r[...] for r in rows_p], [scr[...] for scr in w_scr]).astype(op_ref.dtype)


def dense(name, body, rows_p, rows_s, weights, layer, n_out, tm, tn, out_dtype):
    m_p, m_s = rows_p[0][0].shape[0], rows_s[0][0].shape[0]
    tm = min(tm, m_p)

    def row_specs(rows, t, tiled):
        specs = []
        for arr, col in rows:
            if col is None:
                specs.append(pl.BlockSpec((t, arr.shape[1]), (lambda j, i: (i, 0)) if tiled else (lambda j, i: (0, 0))))
            elif tiled:
                specs.append(pl.BlockSpec((t, tn), lambda j, i, col=col: (i, col + j)))
            else:
                specs.append(pl.BlockSpec((t, tn), lambda j, i, col=col: (0, col + j)))
        return specs

    w_specs = [pl.BlockSpec((None, w.shape[1], tn), lambda j, i, col=col: (layer, 0, col + j)) for w, col in weights]
    return pl.pallas_call(
        functools.partial(_dense_kernel, body=body, n_rows=len(rows_p), n_w=len(weights)),
        grid=(n_out // tn, m_p // tm),
        in_specs=row_specs(rows_p, tm, True) + row_specs(rows_s, m_s, False) + w_specs,
        out_specs=[pl.BlockSpec((tm, tn), lambda j, i: (i, j)), pl.BlockSpec((m_s, tn), lambda j, i: (0, j))],
        out_shape=[jax.ShapeDtypeStruct((m_p, n_out), out_dtype), jax.ShapeDtypeStruct((m_s, n_out), out_dtype)],
        scratch_shapes=[pltpu.VMEM((w.shape[1], tn), BF16) for w, _ in weights],
        compiler_params=_params("parallel", "arbitrary"),
        name=name,
    )(*[a for a, _ in rows_p], *[a for a, _ in rows_s], *[w for w, _ in weights])


def _swiglu_body(rows, ws):
    (h,), (wg, wu) = rows, ws
    g = _dot(h, wg)
    return g * _sigmoid(g) * _dot(h, wu)


def _residual_body(rows, ws, *, scale):
    (a, x), (w,) = rows, ws
    return x + scale * _dot(a, w)


def _proj_body(rows, ws):
    return _dot(rows[0], ws[0])


def _merge_body(rows, ws):
    (ya, yb, yc, ga, gb, gc), (wr, wc, wsb) = rows, ws
    return _sigmoid(ga) * _dot(ya, wr) + _sigmoid(gb) * _dot(yb, wc) + _sigmoid(gc) * _dot(yc, wsb)


def _ple_body(rows, ws):
    (h, p, x), (wg, wu) = rows, ws
    return x + _sigmoid(_dot(h, wg)) * _dot(p.astype(BF16), wu)


def _rope(x, cos2, sin2):
    return x * cos2 + pltpu.roll(x, HEAD_DIM // 2, 1) * sin2


def _softplus_neg_abs(z):
    return jnp.log(1.0 + jnp.exp(-jnp.abs(z)))


def _split_bf16(x):
    hi = x.astype(BF16)
    return hi, (x - hi.astype(F32)).astype(BF16)


def _retention_kernel(lg_ref, q_ref, k_ref, v_ref, cos_ref, sin_ref, gn_ref, ya_ref, s_out_ref, s_scr,
                      *, n_chunks):
    c = pl.program_id(1)

    @pl.when(c == 0)
    def _():
        s_scr[...] = jnp.zeros_like(s_scr)

    cos2 = cos_ref[...]
    sin2 = sin_ref[...]
    ii = lax.broadcasted_iota(jnp.int32, (CHUNK, CHUNK), 0)
    jj = lax.broadcasted_iota(jnp.int32, (CHUNK, CHUNK), 1)
    diff = (ii - jj).astype(F32)
    pos = lax.broadcasted_iota(jnp.int32, (CHUNK, 1), 0).astype(F32)
    for h in range(HEADS):
        cols = slice(h * HEAD_DIM, (h + 1) * HEAD_DIM)
        lg = lg_ref[h]
        q = _rope(q_ref[:, cols], cos2, sin2)
        k = _rope(k_ref[:, cols], cos2, sin2) * (HEAD_DIM ** -0.5)
        v = v_ref[:, cols].astype(BF16)
        qb = q.astype(BF16)
        dmask = jnp.where(diff >= 0, jnp.exp(lg * jnp.maximum(diff, 0.0)), 0.0)
        scores = _dot_nt(qb, k.astype(BF16)) * dmask
        inner = _dot(scores.astype(BF16), v)
        s = s_scr[h]
        cross = _dot(qb, s.astype(BF16)) * jnp.exp(lg * (pos + 1.0))
        kd = (k * jnp.exp(lg * (CHUNK - 1.0 - pos))).astype(BF16)
        s_scr[h] = jnp.exp(lg * CHUNK) * s + _dot_tn(kd, v)
        o = inner + cross
        ya_ref[:, cols] = _rms(o, gn_ref[h:h + 1, :]).astype(ya_ref.dtype)

    @pl.when(c == n_chunks - 1)
    def _():
        s_out_ref[...] = s_scr[...]


def retention_prompt(proj, cos2, sin2, lg, gn, layer, batch, seq):
    n_chunks = seq // CHUNK
    m = batch * seq

    def col_spec(col):
        return pl.BlockSpec((CHUNK, MIX_W), lambda b, c, lg_ref: (b * n_chunks + c, col))

    tab_spec = pl.BlockSpec((CHUNK, HEAD_DIM), lambda b, c, lg_ref: (c, 0))
    return pl.pallas_call(
        functools.partial(_retention_kernel, n_chunks=n_chunks),
        grid_spec=pltpu.PrefetchScalarGridSpec(
            num_scalar_prefetch=1,
            grid=(batch, n_chunks),
            in_specs=[col_spec(COL_RQ), col_spec(COL_RK), col_spec(COL_RV), tab_spec, tab_spec,
                      pl.BlockSpec((None, HEADS, HEAD_DIM), lambda b, c, lg_ref: (layer, 0, 0))],
            out_specs=[pl.BlockSpec((CHUNK, MIX_W), lambda b, c, lg_ref: (b * n_chunks + c, 0)),
                       pl.BlockSpec((None, HEADS, HEAD_DIM, HEAD_DIM), lambda b, c, lg_ref: (b, 0, 0, 0))],
            scratch_shapes=[pltpu.VMEM((HEADS, HEAD_DIM, HEAD_DIM), F32)]),
        out_shape=[jax.ShapeDtypeStruct((m, MIX_W), BF16),
                   jax.ShapeDtypeStruct((batch, HEADS, HEAD_DIM, HEAD_DIM), F32)],
        compiler_params=_params("parallel", "arbitrary"),
        name="retention_prompt",
    )(lg, proj, proj, proj, cos2, sin2, gn)


def _conv_kernel(ch_ref, cb_ref, cc_ref, w_ref, yb_ref, cn_ref):
    u = cc_ref[...] * ch_ref[...]
    seq = u.shape[0]
    t = lax.broadcasted_iota(jnp.int32, u.shape, 0)
    u1 = jnp.where(t >= 1, pltpu.roll(u, 1, 0), 0.0)
    u2 = jnp.where(t >= 2, pltpu.roll(u, 2, 0), 0.0)
    cv = u2 * w_ref[0:1, :] + u1 * w_ref[1:2, :] + u * w_ref[2:3, :]
    yb_ref[...] = (cb_ref[...] * cv).astype(yb_ref.dtype)
    cn_ref[...] = cc_ref[seq - 2:seq, :] * ch_ref[seq - 2:seq, :]


def conv_prompt(proj, conv_w, layer, batch, seq):
    tw = 256
    nw = MIX_W // tw

    def col_spec(col):
        return pl.BlockSpec((seq, tw), lambda b, j: (b, col * nw + j))

    return pl.pallas_call(
        _conv_kernel,
        grid=(batch, nw),
        in_specs=[col_spec(COL_CH), col_spec(COL_CB), col_spec(COL_CC),
                  pl.BlockSpec((None, 3, tw), lambda b, j: (layer, 0, j))],
        out_specs=[pl.BlockSpec((seq, tw), lambda b, j: (b, j)),
                   pl.BlockSpec((None, 2, tw), lambda b, j: (b, 0, j))],
        out_shape=[jax.ShapeDtypeStruct((batch * seq, MIX_W), BF16),
                   jax.ShapeDtypeStruct((batch, 2, MIX_W), F32)],
        compiler_params=_params("parallel", "parallel"),
        name="conv_prompt",
    )(proj, proj, proj, conv_w)


def _head_norm_kernel(x_ref, g_ref, o_ref):
    g = g_ref[...]
    for h in range(HEADS):
        cols = slice(h * HEAD_DIM, (h + 1) * HEAD_DIM)
        o_ref[:, cols] = _rms(x_ref[:, cols], g)


def head_norm(proj, g, col, layer, tm):
    m = proj.shape[0]
    return pl.pallas_call(
        _head_norm_kernel,
        grid=(m // tm,),
        in_specs=[pl.BlockSpec((tm, MIX_W), lambda i: (i, col)),
                  pl.BlockSpec((None, 1, HEAD_DIM), lambda i: (layer, 0, 0))],
        out_specs=pl.BlockSpec((tm, MIX_W), lambda i: (i, 0)),
        out_shape=jax.ShapeDtypeStruct((m, MIX_W), F32),
        compiler_params=_params("parallel"),
        name="head_norm",
    )(proj, g.reshape(DEPTH, 1, HEAD_DIM))


SB_TILE = 256


SB_HEADS_PER_STEP = 4


def _sb_prompt_kernel(bias_ref, q_ref, k_ref, v_ref, qg_ref, o_ref, acc_scr, run_scr):
    t = SB_TILE
    hg = pl.program_id(1)
    qi = pl.program_id(2)
    row = lax.broadcasted_iota(jnp.int32, (t, t), 0)
    col = lax.broadcasted_iota(jnp.int32, (t, t), 1)
    causal = col < row
    r2 = lax.broadcasted_iota(jnp.int32, (t, t + HEAD_DIM), 0)
    c2 = lax.broadcasted_iota(jnp.int32, (t, t + HEAD_DIM), 1)
    scan = jnp.where((r2 > c2) | (c2 >= t), 1.0, 0.0).astype(BF16)
    acc_scr[...] = jnp.zeros_like(acc_scr)
    run_scr[...] = jnp.zeros_like(run_scr)
    heads = range(SB_HEADS_PER_STEP)
    cols = [slice(hh * HEAD_DIM, (hh + 1) * HEAD_DIM) for hh in heads]
    qn = [_rms(q_ref[:, cols[hh]], qg_ref[...]).astype(BF16) for hh in heads]
    bias = [bias_ref[hg * SB_HEADS_PER_STEP + hh] for hh in heads]

    def block(kj, diagonal):
        start = pl.multiple_of(kj * t, t)
        for hh in heads:
            kb = k_ref[pl.ds(start, t), cols[hh]].astype(BF16)
            vb = v_ref[pl.ds(start, t), cols[hh]].astype(BF16)
            z = _dot_nt(qn[hh], kb) * (HEAD_DIM ** -0.5) + bias[hh]
            sp = _softplus_neg_abs(z)
            log_b = jnp.minimum(z, 0.0) - sp
            log_1mb = -jnp.maximum(z, 0.0) - sp
            if diagonal:
                log_1mb = jnp.where(causal, log_1mb, 0.0)
            hi, lo = _split_bf16(log_1mb)
            r = _dot(hi, scan) + _dot(lo, scan)
            run = run_scr[hh]
            e = jnp.exp(log_b + r[:, :t] + jnp.concatenate([run] * (t // HEAD_DIM), axis=1))
            if diagonal:
                e = jnp.where(causal, e, 0.0)
            acc_scr[hh] += _dot(e.astype(BF16), vb)
            run_scr[hh] = run + r[:, t:]

    block(qi, True)

    def body(i, carry):
        block(qi - 1 - i, False)
        return carry

    lax.fori_loop(0, qi, body, 0)
    for hh in heads:
        o_ref[:, cols[hh]] = acc_scr[hh].astype(o_ref.dtype)


def sb_prompt(proj, sk, q_gain, bias, layer, batch, seq):
    t = SB_TILE
    nq = seq // t
    hs = SB_HEADS_PER_STEP
    width = hs * HEAD_DIM
    groups = HEADS // hs
    return pl.pallas_call(
        _sb_prompt_kernel,
        grid_spec=pltpu.PrefetchScalarGridSpec(
            num_scalar_prefetch=1,
            grid=(batch, groups, nq),
            in_specs=[pl.BlockSpec((t, width), lambda b, g, qi, s: (b * nq + qi, COL_SQ * groups + g)),
                      pl.BlockSpec((seq, width), lambda b, g, qi, s: (b, g)),
                      pl.BlockSpec((seq, width), lambda b, g, qi, s: (b, COL_SV * groups + g)),
                      pl.BlockSpec((None, 1, HEAD_DIM), lambda b, g, qi, s: (layer, 0, 0))],
            out_specs=pl.BlockSpec((t, width), lambda b, g, qi, s: (b * nq + qi, g)),
            scratch_shapes=[pltpu.VMEM((hs, t, HEAD_DIM), F32), pltpu.VMEM((hs, t, HEAD_DIM), F32)]),
        out_shape=jax.ShapeDtypeStruct((batch * seq, MIX_W), BF16),
        compiler_params=_params("parallel", "parallel", "parallel"),
        name="sb_prompt",
    )(bias, proj, sk, proj, q_gain.reshape(DEPTH, 1, HEAD_DIM))


def _decode_mix_kernel(lg_ref, p_ref, s_ref, cprev_ref, cos_ref, sin_ref, gn_ref, cw_ref, qg_ref, kg_ref,
                       ya_ref, yb_ref, sq_ref, sk_ref, s_out_ref, cn_ref):
    cos2 = cos_ref[...]
    sin2 = sin_ref[...]
    row = lax.broadcasted_iota(jnp.int32, (HEAD_DIM, HEAD_DIM), 0)
    col = lax.broadcasted_iota(jnp.int32, (HEAD_DIM, HEAD_DIM), 1)

    def stream(c, h):
        return p_ref[:, c * MIX_W + h * HEAD_DIM:c * MIX_W + (h + 1) * HEAD_DIM]

    def bf_round(x):
        return x.astype(BF16).astype(F32)

    for h in range(HEADS):
        cols = slice(h * HEAD_DIM, (h + 1) * HEAD_DIM)
        lg = lg_ref[h]
        q = bf_round(_rope(stream(COL_RQ, h), cos2, sin2))
        k = bf_round(_rope(stream(COL_RK, h), cos2, sin2) * (HEAD_DIM ** -0.5))
        v = bf_round(stream(COL_RV, h))
        s = s_ref[h]
        score = bf_round(jnp.sum(q * k, axis=-1, keepdims=True))
        inner = score * v
        q_rows = jnp.broadcast_to(q, (16, HEAD_DIM)).astype(BF16)
        cross = _dot(q_rows, s.astype(BF16))[0:1, :] * jnp.exp(lg * jnp.ones((1, 1), F32))
        k_col = jnp.sum(jnp.where(row == col, jnp.broadcast_to(k, (HEAD_DIM, HEAD_DIM)), 0.0),
                        axis=1, keepdims=True)
        s_out_ref[h] = jnp.exp(lg * jnp.ones((1, 1), F32)) * s + k_col * v
        ya_ref[:, cols] = _rms(inner + cross, gn_ref[h:h + 1, :]).astype(ya_ref.dtype)
        sq_ref[:, cols] = _rms(stream(COL_SQ, h), qg_ref[...])
        sk_ref[:, cols] = _rms(stream(COL_SK, h), kg_ref[...])

    ch = p_ref[:, COL_CH * MIX_W:(COL_CH + 1) * MIX_W]
    cb = p_ref[:, COL_CB * MIX_W:(COL_CB + 1) * MIX_W]
    cc = p_ref[:, COL_CC * MIX_W:(COL_CC + 1) * MIX_W]
    u = cc * ch
    cv = cprev_ref[0:1, :] * cw_ref[0:1, :] + cprev_ref[1:2, :] * cw_ref[1:2, :] + u * cw_ref[2:3, :]
    yb_ref[...] = (cb * cv).astype(yb_ref.dtype)
    cn_ref[0:1, :] = cprev_ref[1:2, :]
    cn_ref[1:2, :] = u


def decode_mix(proj_s, state_ret, state_conv, cos2, sin2, lg, gn, conv_w, q_gain, k_gain, layer, nb):
    def row_spec(width):
        return pl.BlockSpec((None, 1, width), lambda b, s: (b, 0, 0))

    def w_spec(*shape):
        return pl.BlockSpec((None,) + shape, lambda b, s: (layer,) + (0,) * len(shape))

    tab_spec = pl.BlockSpec((1, HEAD_DIM), lambda b, s: (0, 0))
    return pl.pallas_call(
        _decode_mix_kernel,
        grid_spec=pltpu.PrefetchScalarGridSpec(
            num_scalar_prefetch=1,
            grid=(nb,),
            in_specs=[row_spec(N_IN),
                      pl.BlockSpec((None, None, HEADS, HEAD_DIM, HEAD_DIM), lambda b, s: (layer, b, 0, 0, 0)),
                      pl.BlockSpec((None, None, 2, MIX_W), lambda b, s: (layer, b, 0, 0)),
                      tab_spec, tab_spec, w_spec(HEADS, HEAD_DIM), w_spec(3, MIX_W),
                      w_spec(1, HEAD_DIM), w_spec(1, HEAD_DIM)],
            out_specs=[row_spec(MIX_W), row_spec(MIX_W), row_spec(MIX_W), row_spec(MIX_W),
                       pl.BlockSpec((None, HEADS, HEAD_DIM, HEAD_DIM), lambda b, s: (b, 0, 0, 0)),
                       pl.BlockSpec((None, 2, MIX_W), lambda b, s: (b, 0, 0))]),
        out_shape=[jax.ShapeDtypeStruct((nb, 1, MIX_W), BF16),
                   jax.ShapeDtypeStruct((nb, 1, MIX_W), BF16),
                   jax.ShapeDtypeStruct((nb, 1, MIX_W), F32),
                   jax.ShapeDtypeStruct((nb, 1, MIX_W), F32),
                   jax.ShapeDtypeStruct((nb, HEADS, HEAD_DIM, HEAD_DIM), F32),
                   jax.ShapeDtypeStruct((nb, 2, MIX_W), F32)],
        compiler_params=_params("parallel"),
        name="decode_mix",
    )(lg, proj_s, state_ret, state_conv, cos2, sin2, gn, conv_w,
      q_gain.reshape(DEPTH, 1, HEAD_DIM), k_gain.reshape(DEPTH, 1, HEAD_DIM))


PAGES_PER_STEP = 4
WINDOW = PAGES_PER_STEP * PAGE_SIZE
LANES = 128
SUB_ROWS = 16


def _sb_decode_kernel(pt_ref, q_ref, bias_ref, *refs, n_steps):
    g_n = PAGES_PER_STEP
    k_refs, v_refs = refs[:g_n], refs[g_n:2 * g_n]
    o_ref, qb_scr, scan_scr, acc_scr, run_scr = refs[2 * g_n:]
    s = pl.program_id(1)
    head_row = lax.broadcasted_iota(jnp.int32, (SUB_ROWS, WINDOW), 0)

    def head_window(page_refs, h):
        return jnp.concatenate(
            [r[pl.ds(h, PAGE_SIZE, stride=HEADS), :].astype(BF16) for r in page_refs], axis=0)

    @pl.when(s == 0)
    def _():
        acc_scr[...] = jnp.zeros_like(acc_scr)
        run_scr[...] = jnp.zeros_like(run_scr)
        q = q_ref[...]
        for h in range(HEADS):
            qb_scr[:, h * HEAD_DIM:(h + 1) * HEAD_DIM] = jnp.where(head_row[:, :HEAD_DIM] == h, q, 0.0).astype(BF16)
        r2 = lax.broadcasted_iota(jnp.int32, (WINDOW, WINDOW + LANES), 0)
        c2 = lax.broadcasted_iota(jnp.int32, (WINDOW, WINDOW + LANES), 1)
        scan_scr[...] = jnp.where((r2 > c2) | (c2 >= WINDOW), 1.0, 0.0).astype(BF16)

    k_all = jnp.concatenate([head_window(k_refs, h) for h in range(HEADS)], axis=1)
    z = _dot_nt(qb_scr[...], k_all) * (HEAD_DIM ** -0.5) + bias_ref[...]
    sp = _softplus_neg_abs(z)
    log_b = jnp.minimum(z, 0.0) - sp
    log_1mb = -jnp.maximum(z, 0.0) - sp
    hi, lo = _split_bf16(log_1mb)
    r = _dot(jnp.concatenate([hi, lo], axis=0), scan_scr[...])
    r = r[:SUB_ROWS, :] + r[SUB_ROWS:, :]
    run = run_scr[...]
    a = jnp.exp(log_b + r[:, :WINDOW] + jnp.concatenate([run] * (WINDOW // LANES), axis=1))
    run_scr[...] = run + r[:, WINDOW:]

    a_all = jnp.concatenate([jnp.where(head_row == h, a, 0.0).astype(BF16) for h in range(HEADS)], axis=1)
    v_all = jnp.concatenate([head_window(v_refs, h) for h in range(HEADS)], axis=0)
    acc = acc_scr[...] + _dot(a_all, v_all)
    acc_scr[...] = acc

    @pl.when(s == n_steps - 1)
    def _():
        o_ref[...] = acc[0:HEADS, :]


def sb_decode(q_rows, bias_col, cache_k, cache_v, page_table, layer):
    nb, n_pages = page_table.shape
    n_steps = n_pages // PAGES_PER_STEP

    def page_spec(g):
        return pl.BlockSpec(
            (None, None, PAGE_SIZE * HEADS, HEAD_DIM),
            lambda b, s, pt: (layer, pt[b, n_pages - (s + 1) * PAGES_PER_STEP + g], 0, 0))

    pages = [page_spec(g) for g in range(PAGES_PER_STEP)]
    return pl.pallas_call(
        functools.partial(_sb_decode_kernel, n_steps=n_steps),
        grid_spec=pltpu.PrefetchScalarGridSpec(
            num_scalar_prefetch=1,
            grid=(nb, n_steps),
            in_specs=[pl.BlockSpec((None, SUB_ROWS, HEAD_DIM), lambda b, s, pt: (b, 0, 0)),
                      pl.BlockSpec((None, SUB_ROWS, 1), lambda b, s, pt: (layer, 0, 0))] + pages + pages,
            out_specs=pl.BlockSpec((None, HEADS, HEAD_DIM), lambda b, s, pt: (b, 0, 0)),
            scratch_shapes=[pltpu.VMEM((SUB_ROWS, HEADS * HEAD_DIM), BF16),
                            pltpu.VMEM((WINDOW, WINDOW + LANES), BF16),
                            pltpu.VMEM((SUB_ROWS, HEAD_DIM), F32),
                            pltpu.VMEM((SUB_ROWS, LANES), F32)]),
        out_shape=jax.ShapeDtypeStruct((nb, HEADS, HEAD_DIM), F32),
        compiler_params=_params("parallel", "arbitrary"),
        name="sb_decode",
    )(page_table, q_rows, bias_col, *([cache_k] * PAGES_PER_STEP), *([cache_v] * PAGES_PER_STEP))


def _rope_tables(pos):
    half = HEAD_DIM // 2
    inv = ROPE_THETA ** (-jnp.arange(half, dtype=F32) / half)
    ang = pos.astype(F32)[:, None] * inv[None, :]
    cos, sin = jnp.cos(ang), jnp.sin(ang)
    return jnp.concatenate([cos, cos], axis=-1), jnp.concatenate([-sin, sin], axis=-1)


def kernel(x_prompt, x_sample, cache_sb_k, cache_sb_v, state_ret, state_conv, page_table, p_prompt, p_sample,
           ffn1_norm, ffn1_w_gu, ffn1_w_down, mix_norm, w_in, ret_gn, conv_w, sb_q_norm, sb_k_norm, sb_bias,
           w_branch_ret, w_branch_conv, w_branch_sb, w_out, ffn2_norm, ffn2_w_gu, ffn2_w_down,
           ple_norm, w_ple_gate, w_ple_up):
    batch, seq, _ = x_prompt.shape
    nb = x_sample.shape[0]
    m_p = batch * seq
    m_s = 16

    lg =jnp.log1p(-jnp.exp2(-5.0 - jnp.arange(HEADS, dtype=F32)))
    cos_p, sin_p = _rope_tables(jnp.arange(seq, dtype=jnp.int32))
    past_len = page_table.shape[1] * PAGE_SIZE
    cos_s, sin_s = _rope_tables(jnp.full((1,), past_len, jnp.int32))
    tp = lambda t: min(t, m_p)
    page_rows = cache_sb_k.shape[:2] + (PAGE_SIZE * HEADS, HEAD_DIM)
    cache_sb_k, cache_sb_v = cache_sb_k.reshape(page_rows), cache_sb_v.reshape(page_rows)
    row_pad = ((0, 0), (0, SUB_ROWS - HEADS), (0, 0))
    bias_cols = jnp.pad(sb_bias.reshape(DEPTH, HEADS, 1), row_pad)

    xp = x_prompt.reshape(m_p, D_MODEL)
    xs = jnp.pad(x_sample.reshape(nb, D_MODEL), ((0, m_s - nb), (0, 0)))
    pp = p_prompt.reshape(DEPTH, m_p, P_DIM)
    ps = jnp.pad(p_sample.reshape(DEPTH, nb, P_DIM), ((0, 0), (0, m_s - nb), (0, 0)))

    tn = 512
    gate_col = MIX_W // tn

    def ffn(xp, xs, norm, w_gu, w_down, i):
        hp, hs = rmsnorm_bf16(xp, xs, norm, i, tp(256))
        ap, a_s = dense("swiglu_mm", _swiglu_body, [(hp, None)], [(hs, None)],
                        [(w_gu, 0), (w_gu, D_FF // tn)], i, D_FF, 1024, tn, BF16)
        return dense("ffn_down", functools.partial(_residual_body, scale=0.5),
                     [(ap, None), (xp, 0)], [(a_s, None), (xs, 0)], [(w_down, 0)], i, D_MODEL, 512, tn, F32)

    outs = [[] for _ in range(8)]
    for i in range(DEPTH):
        xp, xs = ffn(xp, xs, ffn1_norm, ffn1_w_gu, ffn1_w_down, i)
        hp, hs = rmsnorm_bf16(xp, xs, mix_norm, i, tp(256))
        proj, proj_s = dense("in_proj", _proj_body, [(hp, None)], [(hs, None)], [(w_in, 0)],
                             i, N_IN, 1024, 1024, F32)

        ya, s_p = retention_prompt(proj, cos_p, sin_p, lg, ret_gn, i, batch, seq)
        yb, c_p = conv_prompt(proj, conv_w, i, batch, seq)
        sk_p = head_norm(proj, sb_k_norm, COL_SK, i, tp(512))
        yc = sb_prompt(proj, sk_p, sb_q_norm, sb_bias[i], i, batch, seq)
        sv_p = proj[:, COL_SV * MIX_W:(COL_SV + 1) * MIX_W]

        ya_s, yb_s, sq_s, sk_s, s_s, c_s = decode_mix(
            proj_s[:nb].reshape(nb, 1, N_IN), state_ret, state_conv, cos_s, sin_s, lg, ret_gn, conv_w,
            sb_q_norm, sb_k_norm, i, nb)
        q_rows = jnp.pad(sq_s.reshape(nb, HEADS, HEAD_DIM), row_pad)
        yc_s = sb_decode(q_rows, bias_cols, cache_sb_k, cache_sb_v, page_table, i)
        yc_s = yc_s.reshape(nb, MIX_W).astype(BF16)
        pad_rows = lambda y: jnp.pad(y.reshape(nb, MIX_W), ((0, m_s - nb), (0, 0)))
        sv_s = proj_s[:nb, COL_SV * MIX_W:(COL_SV + 1) * MIX_W]

        def branch_rows(ya, yb, yc, proj):
            return [(ya, None), (yb, None), (yc, None),
                    (proj, COL_GA * gate_col), (proj, COL_GB * gate_col), (proj, COL_GC * gate_col)]

        mp, ms = dense("merge_branches", _merge_body, branch_rows(ya, yb, yc, proj),
                       branch_rows(pad_rows(ya_s), pad_rows(yb_s), pad_rows(yc_s), proj_s),
                       [(w_branch_ret, 0), (w_branch_conv, 0), (w_branch_sb, 0)], i, D_MODEL, 512, tn, BF16)
        xp, xs = dense("out_proj", functools.partial(_residual_body, scale=1.0),
                       [(mp, None), (xp, 0)], [(ms, None), (xs, 0)], [(w_out, 0)], i, D_MODEL, 1024, tn, F32)
        xp, xs = ffn(xp, xs, ffn2_norm, ffn2_w_gu, ffn2_w_down, i)
        hp, hs = rmsnorm_bf16(xp, xs, ple_norm, i, tp(256))
        xp, xs = dense("ple_update", _ple_body, [(hp, None), (pp[i], None), (xp, 0)],
                       [(hs, None), (ps[i], None), (xs, 0)], [(w_ple_gate, 0), (w_ple_up, 0)],
                       i, D_MODEL, 1024, tn, F32)

        for lst, val in zip(outs, (sk_p.reshape(batch, seq, HEADS, HEAD_DIM),
                                   sv_p.reshape(batch, seq, HEADS, HEAD_DIM),
                                   sk_s.reshape(nb, 1, HEADS, HEAD_DIM),
                                   sv_s.reshape(nb, 1, HEADS, HEAD_DIM),
                                   s_p, s_s, c_p, c_s)):
            lst.append(val)

    return (xp.reshape(batch, seq, D_MODEL), xs[:nb].reshape(nb, 1, D_MODEL)) + tuple(jnp.stack(o) for o in outs)
```

```python
import functools

import jax
import jax.numpy as jnp
from jax import lax
from jax.experimental import pallas as pl
from jax.experimental.pallas import tpu as pltpu

F32 = jnp.float32
BF16 = jnp.bfloat16

D_MODEL = 2048
DEPTH = 2
PAGE_SIZE = 128
HEADS = 8
HEAD_DIM = 128
MIX_W = HEADS * HEAD_DIM
CHUNK = 128
D_FF = 5632
P_DIM = 256
ROPE_THETA = 10000.0
EPS = 1e-6
N_IN = 9 * MIX_W + 3 * D_MODEL
COL_RQ, COL_RK, COL_RV, COL_CH, COL_CB, COL_CC, COL_SQ, COL_SK, COL_SV = range(9)
COL_GA, COL_GB, COL_GC = 9, 11, 13

VMEM_LIMIT_BYTES = 56 * 1024 * 1024


def _params(*sem):
    return pltpu.CompilerParams(dimension_semantics=sem, vmem_limit_bytes=VMEM_LIMIT_BYTES)


def _dot(a, b):
    return jnp.dot(a, b, preferred_element_type=F32)


def _dot_nt(a, b):
    return lax.dot_general(a, b, (((1,), (1,)), ((), ())), preferred_element_type=F32)


def _dot_tn(a, b):
    return lax.dot_general(a, b, (((0,), (0,)), ((), ())), preferred_element_type=F32)


def _rms(x, g):
    return x * lax.rsqrt(jnp.mean(x * x, axis=-1, keepdims=True) + EPS) * g


def _sigmoid(x):
    return jax.nn.sigmoid(x)


def _rmsnorm_kernel(xp_ref, xs_ref, g_ref, op_ref, os_ref):
    op_ref[...] = _rms(xp_ref[...], g_ref[...]).astype(op_ref.dtype)

    @pl.when(pl.program_id(0) == 0)
    def _():
        os_ref[...] = _rms(xs_ref[...], g_ref[...]).astype(os_ref.dtype)


def rmsnorm_bf16(xp, xs, g, layer, tm):
    m, d = xp.shape
    ms = xs.shape[0]
    return pl.pallas_call(
        _rmsnorm_kernel,
        grid=(m // tm,),
        in_specs=[pl.BlockSpec((tm, d), lambda i: (i, 0)),
                  pl.BlockSpec((ms, d), lambda i: (0, 0)),
                  pl.BlockSpec((None, 1, d), lambda i: (layer, 0, 0))],
        out_specs=[pl.BlockSpec((tm, d), lambda i: (i, 0)),
                   pl.BlockSpec((ms, d), lambda i: (0, 0))],
        out_shape=[jax.ShapeDtypeStruct((m, d), BF16), jax.ShapeDtypeStruct((ms, d), BF16)],
        compiler_params=_params("arbitrary"),
        name="rmsnorm",
    )(xp, xs, g.reshape(DEPTH, 1, d))


def _dense_kernel(*refs, body, n_rows, n_w):
    rows_p, rows_s = refs[:n_rows], refs[n_rows:2 * n_rows]
    first_refs = refs[2 * n_rows:2 * n_rows + n_w]
    next_refs = refs[2 * n_rows + n_w:2 * n_rows + 2 * n_w]
    op_ref, os_ref = refs[2 * n_rows + 2 * n_w:2 * n_rows + 2 * n_w + 2]
    w_scr = refs[2 * n_rows + 2 * n_w + 2:]
    j, i = pl.program_id(0), pl.program_id(1)
    cur = j % 2

    @pl.when((j == 0) & (i == 0))
    def _():
        for first, scr in zip(first_refs, w_scr):
            scr[0] = first[...].astype(BF16)

    for nxt, scr in zip(next_refs, w_scr):
        rows = nxt.shape[0]
        scr[1 - cur, pl.ds(pl.multiple_of(i * rows, rows), rows), :] = nxt[...].astype(BF16)

    @pl.when(i == 0)
    def _():
        os_ref[...] = body([r[...] for r in rows_s], [scr[cur] for scr in w_scr]).astype(os_ref.dtype)

    op_ref[...] = body([r[...] for r in rows_p], [scr[cur] for scr in w_scr]).astype(op_ref.dtype)


DENSE_TILES = {
    "swiglu_mm": (1024, 512), "ffn_down": (512, 512), "in_proj": (1024, 1024),
    "merge_branches": (256, 1024), "out_proj": (512, 1024), "ple_update": (512, 1024),
}


def dense(name, body, rows_p, rows_s, weights, layer, n_out, out_dtype):
    tm, tn = DENSE_TILES[name]
    rows_p = [(a, c if c is None else c // tn) for a, c in rows_p]
    rows_s = [(a, c if c is None else c // tn) for a, c in rows_s]
    weights = [(a, c // tn) for a, c in weights]
    m_p, m_s = rows_p[0][0].shape[0], rows_s[0][0].shape[0]
    tm = min(tm, m_p)

    def row_specs(rows, t, tiled):
        specs = []
        for arr, col in rows:
            if col is None:
                specs.append(pl.BlockSpec((t, arr.shape[1]), (lambda j, i: (i, 0)) if tiled else (lambda j, i: (0, 0))))
            elif tiled:
                specs.append(pl.BlockSpec((t, tn), lambda j, i, col=col: (i, col + j)))
            else:
                specs.append(pl.BlockSpec((t, tn), lambda j, i, col=col: (0, col + j)))
        return specs

    n_j, n_i = n_out // tn, m_p // tm
    first_specs = [pl.BlockSpec((None, w.shape[1], tn), lambda j, i, col=col: (layer, 0, col),
                                pipeline_mode=pl.Buffered(1)) for w, col in weights]
    for w, _ in weights:
        assert w.shape[1] % (n_i * SUB_ROWS) == 0, (name, w.shape, n_i)
    next_specs = [pl.BlockSpec((None, w.shape[1] // n_i, tn),
                               lambda j, i, col=col: (layer, i, col + jnp.minimum(j + 1, n_j - 1)))
                  for w, col in weights]
    return pl.pallas_call(
        functools.partial(_dense_kernel, body=body, n_rows=len(rows_p), n_w=len(weights)),
        grid=(n_j, n_i),
        in_specs=row_specs(rows_p, tm, True) + row_specs(rows_s, m_s, False) + first_specs + next_specs,
        out_specs=[pl.BlockSpec((tm, tn), lambda j, i: (i, j)), pl.BlockSpec((m_s, tn), lambda j, i: (0, j))],
        out_shape=[jax.ShapeDtypeStruct((m_p, n_out), out_dtype), jax.ShapeDtypeStruct((m_s, n_out), out_dtype)],
        scratch_shapes=[pltpu.VMEM((2, w.shape[1], tn), BF16) for w, _ in weights],
        compiler_params=_params("arbitrary", "arbitrary"),
        name=name,
    )(*[a for a, _ in rows_p], *[a for a, _ in rows_s], *[w for w, _ in weights], *[w for w, _ in weights])


def _swiglu_body(rows, ws):
    (h,), (wg, wu) = rows, ws
    g = _dot(h, wg)
    return g * _sigmoid(g) * _dot(h, wu)


def _residual_body(rows, ws, *, scale):
    (a, x), (w,) = rows, ws
    return x + scale * _dot(a, w)


def _proj_body(rows, ws):
    return _dot(rows[0], ws[0])


def _merge_body(rows, ws):
    (ya, yb, yc, ga, gb, gc), (wr, wc, wsb) = rows, ws
    return _sigmoid(ga) * _dot(ya, wr) + _sigmoid(gb) * _dot(yb, wc) + _sigmoid(gc) * _dot(yc, wsb)


def _ple_body(rows, ws):
    (h, p, x), (wg, wu) = rows, ws
    return x + _sigmoid(_dot(h, wg)) * _dot(p.astype(BF16), wu)


def _rope(x, cos2, sin2):
    return x * cos2 + pltpu.roll(x, HEAD_DIM // 2, 1) * sin2


def _softplus_neg_abs(z):
    neg_abs = lax.bitcast_convert_type(
        lax.bitcast_convert_type(z, jnp.uint32) | jnp.uint32(0x80000000), F32)
    return jnp.log(1.0 + jnp.exp(neg_abs))


def _split_bf16(x):
    hi = x.astype(BF16)
    return hi, (x - hi.astype(F32)).astype(BF16)


def _retention_kernel(lg_ref, q_ref, k_ref, v_ref, cos_ref, sin_ref, gn_ref, ya_ref, s_out_ref, s_scr,
                      *, n_chunks):
    c = pl.program_id(1)

    @pl.when(c == 0)
    def _():
        s_scr[...] = jnp.zeros_like(s_scr)

    cos2 = cos_ref[...]
    sin2 = sin_ref[...]
    ii = lax.broadcasted_iota(jnp.int32, (CHUNK, CHUNK), 0)
    jj = lax.broadcasted_iota(jnp.int32, (CHUNK, CHUNK), 1)
    diff = (ii - jj).astype(F32)
    pos = lax.broadcasted_iota(jnp.int32, (CHUNK, 1), 0).astype(F32)
    for h in range(HEADS):
        cols = slice(h * HEAD_DIM, (h + 1) * HEAD_DIM)
        lg = lg_ref[h]
        q = _rope(q_ref[:, cols], cos2, sin2)
        k = _rope(k_ref[:, cols], cos2, sin2) * (HEAD_DIM ** -0.5)
        v = v_ref[:, cols].astype(BF16)
        qb = q.astype(BF16)
        dmask = jnp.where(diff >= 0, jnp.exp(lg * jnp.maximum(diff, 0.0)), 0.0)
        scores = _dot_nt(qb, k.astype(BF16)) * dmask
        inner = _dot(scores.astype(BF16), v)
        s = s_scr[h]
        cross = _dot(qb, s.astype(BF16)) * jnp.exp(lg * (pos + 1.0))
        kd = (k * jnp.exp(lg * (CHUNK - 1.0 - pos))).astype(BF16)
        s_scr[h] = jnp.exp(lg * CHUNK) * s + _dot_tn(kd, v)
        o = inner + cross
        ya_ref[:, cols] = _rms(o, gn_ref[h:h + 1, :]).astype(ya_ref.dtype)

    @pl.when(c == n_chunks - 1)
    def _():
        s_out_ref[...] = s_scr[...]


def retention_prompt(proj, cos2, sin2, lg, gn, layer, batch, seq):
    n_chunks = seq // CHUNK
    m = batch * seq

    def col_spec(col):
        return pl.BlockSpec((CHUNK, MIX_W), lambda b, c, lg_ref: (b * n_chunks + c, col))

    tab_spec = pl.BlockSpec((CHUNK, HEAD_DIM), lambda b, c, lg_ref: (c, 0))
    return pl.pallas_call(
        functools.partial(_retention_kernel, n_chunks=n_chunks),
        grid_spec=pltpu.PrefetchScalarGridSpec(
            num_scalar_prefetch=1,
            grid=(batch, n_chunks),
            in_specs=[col_spec(COL_RQ), col_spec(COL_RK), col_spec(COL_RV), tab_spec, tab_spec,
                      pl.BlockSpec((None, HEADS, HEAD_DIM), lambda b, c, lg_ref: (layer, 0, 0))],
            out_specs=[pl.BlockSpec((CHUNK, MIX_W), lambda b, c, lg_ref: (b * n_chunks + c, 0)),
                       pl.BlockSpec((None, HEADS, HEAD_DIM, HEAD_DIM), lambda b, c, lg_ref: (b, 0, 0, 0))],
            scratch_shapes=[pltpu.VMEM((HEADS, HEAD_DIM, HEAD_DIM), F32)]),
        out_shape=[jax.ShapeDtypeStruct((m, MIX_W), BF16),
                   jax.ShapeDtypeStruct((batch, HEADS, HEAD_DIM, HEAD_DIM), F32)],
        compiler_params=_params("parallel", "arbitrary"),
        name="retention_prompt",
    )(lg, proj, proj, proj, cos2, sin2, gn)


def _conv_kernel(ch_ref, cb_ref, cc_ref, w_ref, yb_ref, cn_ref):
    u = cc_ref[...] * ch_ref[...]
    seq = u.shape[0]
    t = lax.broadcasted_iota(jnp.int32, u.shape, 0)
    u1 = jnp.where(t >= 1, pltpu.roll(u, 1, 0), 0.0)
    u2 = jnp.where(t >= 2, pltpu.roll(u, 2, 0), 0.0)
    cv = u2 * w_ref[0:1, :] + u1 * w_ref[1:2, :] + u * w_ref[2:3, :]
    yb_ref[...] = (cb_ref[...] * cv).astype(yb_ref.dtype)
    cn_ref[...] = cc_ref[seq - 2:seq, :] * ch_ref[seq - 2:seq, :]


def conv_prompt(proj, conv_w, layer, batch, seq):
    tw = 256
    nw = MIX_W // tw

    def col_spec(col):
        return pl.BlockSpec((seq, tw), lambda b, j: (b, col * nw + j))

    return pl.pallas_call(
        _conv_kernel,
        grid=(batch, nw),
        in_specs=[col_spec(COL_CH), col_spec(COL_CB), col_spec(COL_CC),
                  pl.BlockSpec((None, 3, tw), lambda b, j: (layer, 0, j))],
        out_specs=[pl.BlockSpec((seq, tw), lambda b, j: (b, j)),
                   pl.BlockSpec((None, 2, tw), lambda b, j: (b, 0, j))],
        out_shape=[jax.ShapeDtypeStruct((batch * seq, MIX_W), BF16),
                   jax.ShapeDtypeStruct((batch, 2, MIX_W), F32)],
        compiler_params=_params("parallel", "parallel"),
        name="conv_prompt",
    )(proj, proj, proj, conv_w)


def _head_norm_kernel(x_ref, g_ref, o_ref):
    g = g_ref[...]
    for h in range(HEADS):
        cols = slice(h * HEAD_DIM, (h + 1) * HEAD_DIM)
        o_ref[:, cols] = _rms(x_ref[:, cols], g)


def head_norm(proj, g, col, layer, tm):
    m = proj.shape[0]
    return pl.pallas_call(
        _head_norm_kernel,
        grid=(m // tm,),
        in_specs=[pl.BlockSpec((tm, MIX_W), lambda i: (i, col)),
                  pl.BlockSpec((None, 1, HEAD_DIM), lambda i: (layer, 0, 0))],
        out_specs=pl.BlockSpec((tm, MIX_W), lambda i: (i, 0)),
        out_shape=jax.ShapeDtypeStruct((m, MIX_W), F32),
        compiler_params=_params("parallel"),
        name="head_norm",
    )(proj, g.reshape(DEPTH, 1, HEAD_DIM))


SB_TILE = 256


SB_HEADS_PER_STEP = 8


def _sb_prompt_kernel(bias_ref, q_ref, k_ref, v_ref, qg_ref, o_ref, acc_scr, run_scr):
    t = SB_TILE
    w = HEAD_DIM
    hg = pl.program_id(1)
    qi = pl.program_id(2)
    row = lax.broadcasted_iota(jnp.int32, (t, t), 0)
    col = lax.broadcasted_iota(jnp.int32, (t, t), 1)
    causal = col < row
    key = jnp.bitwise_and(lax.broadcasted_iota(jnp.int32, (2 * w, 2 * w), 0), w - 1)
    c2 = lax.broadcasted_iota(jnp.int32, (2 * w, 2 * w), 1)
    scan = jnp.where((key > c2) | (c2 >= w), 1.0, 0.0).astype(BF16)
    acc_scr[...] = jnp.zeros_like(acc_scr)
    run_scr[...] = jnp.zeros_like(run_scr)
    heads = range(SB_HEADS_PER_STEP)
    cols = [slice(hh * HEAD_DIM, (hh + 1) * HEAD_DIM) for hh in heads]
    qn = [_rms(q_ref[:, cols[hh]], qg_ref[...]).astype(BF16) for hh in heads]
    bias = [bias_ref[hg * SB_HEADS_PER_STEP + hh] for hh in heads]

    def block(kj, diagonal):
        start = pl.multiple_of(kj * t, t)
        scores = [_dot_nt(qn[hh], k_ref[pl.ds(start, t), cols[hh]].astype(BF16)) for hh in heads]
        zs, us, scanned = [], [], []
        for hh in heads:
            z = scores[hh] * (HEAD_DIM ** -0.5) + bias[hh]
            u = jnp.maximum(z, 0.0) + _softplus_neg_abs(z)
            if diagonal:
                u = jnp.where(causal, u, 0.0)
            halves = []
            for part in (u[:, :w], u[:, w:]):
                hi, lo = _split_bf16(part)
                halves.append(_dot(jnp.concatenate([hi, lo], axis=1), scan))
            zs.append(z)
            us.append(u)
            scanned.append(halves)
        weights = []
        for hh in heads:
            (left, right), run = scanned[hh], run_scr[hh]
            right_of_left = right[:, w:] + run
            s_left = us[hh][:, :w] + left[:, :w] + right_of_left
            s_right = us[hh][:, w:] + right[:, :w] + run
            e = jnp.exp(zs[hh] - jnp.concatenate([s_left, s_right], axis=1))
            if diagonal:
                e = jnp.where(causal, e, 0.0)
            weights.append(e.astype(BF16))
            run_scr[hh] = left[:, w:] + right_of_left
        for hh in heads:
            acc_scr[hh] += _dot(weights[hh], v_ref[pl.ds(start, t), cols[hh]].astype(BF16))

    block(qi, True)

    def body(i, carry):
        block(qi - 1 - i, False)
        return carry

    lax.fori_loop(0, qi, body, 0)
    for hh in heads:
        o_ref[:, cols[hh]] = acc_scr[hh].astype(o_ref.dtype)


def sb_prompt(proj, sk, q_gain, bias, layer, batch, seq):
    t = SB_TILE
    nq = seq // t
    hs = SB_HEADS_PER_STEP
    width = hs * HEAD_DIM
    groups = HEADS // hs
    return pl.pallas_call(
        _sb_prompt_kernel,
        grid_spec=pltpu.PrefetchScalarGridSpec(
            num_scalar_prefetch=1,
            grid=(batch, groups, nq),
            in_specs=[pl.BlockSpec((t, width), lambda b, g, qi, s: (b * nq + qi, COL_SQ * groups + g)),
                      pl.BlockSpec((seq, width), lambda b, g, qi, s: (b, g)),
                      pl.BlockSpec((seq, width), lambda b, g, qi, s: (b, COL_SV * groups + g)),
                      pl.BlockSpec((None, 1, HEAD_DIM), lambda b, g, qi, s: (layer, 0, 0))],
            out_specs=pl.BlockSpec((t, width), lambda b, g, qi, s: (b * nq + qi, g)),
            scratch_shapes=[pltpu.VMEM((hs, t, HEAD_DIM), F32), pltpu.VMEM((hs, t, HEAD_DIM), F32)]),
        out_shape=jax.ShapeDtypeStruct((batch * seq, MIX_W), BF16),
        compiler_params=_params("parallel", "parallel", "parallel"),
        name="sb_prompt",
    )(bias, proj, sk, proj, q_gain.reshape(DEPTH, 1, HEAD_DIM))


def _decode_mix_kernel(lg_ref, p_ref, s_ref, cprev_ref, cos_ref, sin_ref, gn_ref, cw_ref, qg_ref, kg_ref,
                       ya_ref, yb_ref, sq_ref, sk_ref, s_out_ref, cn_ref):
    cos2 = cos_ref[...]
    sin2 = sin_ref[...]
    row = lax.broadcasted_iota(jnp.int32, (HEAD_DIM, HEAD_DIM), 0)
    col = lax.broadcasted_iota(jnp.int32, (HEAD_DIM, HEAD_DIM), 1)

    def stream(c, h):
        return p_ref[:, c * MIX_W + h * HEAD_DIM:c * MIX_W + (h + 1) * HEAD_DIM]

    def bf_round(x):
        return x.astype(BF16).astype(F32)

    for h in range(HEADS):
        cols = slice(h * HEAD_DIM, (h + 1) * HEAD_DIM)
        lg = lg_ref[h]
        q = bf_round(_rope(stream(COL_RQ, h), cos2, sin2))
        k = bf_round(_rope(stream(COL_RK, h), cos2, sin2) * (HEAD_DIM ** -0.5))
        v = bf_round(stream(COL_RV, h))
        s = s_ref[h]
        score = bf_round(jnp.sum(q * k, axis=-1, keepdims=True))
        inner = score * v
        q_rows = jnp.broadcast_to(q, (16, HEAD_DIM)).astype(BF16)
        cross = _dot(q_rows, s.astype(BF16))[0:1, :] * jnp.exp(lg * jnp.ones((1, 1), F32))
        k_col = jnp.sum(jnp.where(row == col, jnp.broadcast_to(k, (HEAD_DIM, HEAD_DIM)), 0.0),
                        axis=1, keepdims=True)
        s_out_ref[h] = jnp.exp(lg * jnp.ones((1, 1), F32)) * s + k_col * v
        ya_ref[:, cols] = _rms(inner + cross, gn_ref[h:h + 1, :]).astype(ya_ref.dtype)
        sq_ref[:, cols] = _rms(stream(COL_SQ, h), qg_ref[...])
        sk_ref[:, cols] = _rms(stream(COL_SK, h), kg_ref[...])

    ch = p_ref[:, COL_CH * MIX_W:(COL_CH + 1) * MIX_W]
    cb = p_ref[:, COL_CB * MIX_W:(COL_CB + 1) * MIX_W]
    cc = p_ref[:, COL_CC * MIX_W:(COL_CC + 1) * MIX_W]
    u = cc * ch
    cv = cprev_ref[0:1, :] * cw_ref[0:1, :] + cprev_ref[1:2, :] * cw_ref[1:2, :] + u * cw_ref[2:3, :]
    yb_ref[...] = (cb * cv).astype(yb_ref.dtype)
    cn_ref[0:1, :] = cprev_ref[1:2, :]
    cn_ref[1:2, :] = u


def decode_mix(proj_s, state_ret, state_conv, cos2, sin2, lg, gn, conv_w, q_gain, k_gain, layer, nb):
    def row_spec(width):
        return pl.BlockSpec((None, 1, width), lambda b, s: (b, 0, 0))

    def w_spec(*shape):
        return pl.BlockSpec((None,) + shape, lambda b, s: (layer,) + (0,) * len(shape))

    tab_spec = pl.BlockSpec((1, HEAD_DIM), lambda b, s: (0, 0))
    return pl.pallas_call(
        _decode_mix_kernel,
        grid_spec=pltpu.PrefetchScalarGridSpec(
            num_scalar_prefetch=1,
            grid=(nb,),
            in_specs=[row_spec(N_IN),
                      pl.BlockSpec((None, None, HEADS, HEAD_DIM, HEAD_DIM), lambda b, s: (layer, b, 0, 0, 0)),
                      pl.BlockSpec((None, None, 2, MIX_W), lambda b, s: (layer, b, 0, 0)),
                      tab_spec, tab_spec, w_spec(HEADS, HEAD_DIM), w_spec(3, MIX_W),
                      w_spec(1, HEAD_DIM), w_spec(1, HEAD_DIM)],
            out_specs=[row_spec(MIX_W), row_spec(MIX_W), row_spec(MIX_W), row_spec(MIX_W),
                       pl.BlockSpec((None, HEADS, HEAD_DIM, HEAD_DIM), lambda b, s: (b, 0, 0, 0)),
                       pl.BlockSpec((None, 2, MIX_W), lambda b, s: (b, 0, 0))]),
        out_shape=[jax.ShapeDtypeStruct((nb, 1, MIX_W), BF16),
                   jax.ShapeDtypeStruct((nb, 1, MIX_W), BF16),
                   jax.ShapeDtypeStruct((nb, 1, MIX_W), F32),
                   jax.ShapeDtypeStruct((nb, 1, MIX_W), F32),
                   jax.ShapeDtypeStruct((nb, HEADS, HEAD_DIM, HEAD_DIM), F32),
                   jax.ShapeDtypeStruct((nb, 2, MIX_W), F32)],
        compiler_params=_params("parallel"),
        name="decode_mix",
    )(lg, proj_s, state_ret, state_conv, cos2, sin2, gn, conv_w,
      q_gain.reshape(DEPTH, 1, HEAD_DIM), k_gain.reshape(DEPTH, 1, HEAD_DIM))


PAGES_PER_STEP = 8
WINDOW = PAGES_PER_STEP * PAGE_SIZE
LANES = 128
SUB_ROWS = 16


def _sb_decode_kernel(pt_ref, q_ref, bias_ref, *refs, n_steps):
    g_n = PAGES_PER_STEP
    k_refs, v_refs = refs[:g_n], refs[g_n:2 * g_n]
    o_ref, qb_scr, scan_scr, acc_scr, run_scr = refs[2 * g_n:]
    s = pl.program_id(1)
    head_row = lax.broadcasted_iota(jnp.int32, (SUB_ROWS, WINDOW), 0)

    def head_window(page_refs, h):
        return jnp.concatenate(
            [r[pl.ds(h, PAGE_SIZE, stride=HEADS), :].astype(BF16) for r in page_refs], axis=0)

    @pl.when(s == 0)
    def _():
        acc_scr[...] = jnp.zeros_like(acc_scr)
        run_scr[...] = jnp.zeros_like(run_scr)
        q = q_ref[...]
        for h in range(HEADS):
            qb_scr[:, h * HEAD_DIM:(h + 1) * HEAD_DIM] = jnp.where(head_row[:, :HEAD_DIM] == h, q, 0.0).astype(BF16)
        r2 = lax.broadcasted_iota(jnp.int32, (WINDOW, WINDOW + LANES), 0)
        c2 = lax.broadcasted_iota(jnp.int32, (WINDOW, WINDOW + LANES), 1)
        scan_scr[...] = jnp.where((r2 > c2) | (c2 >= WINDOW), 1.0, 0.0).astype(BF16)

    k_all = jnp.concatenate([head_window(k_refs, h) for h in range(HEADS)], axis=1)
    z = _dot_nt(qb_scr[...], k_all) * (HEAD_DIM ** -0.5) + bias_ref[...]
    sp = _softplus_neg_abs(z)
    log_b = jnp.minimum(z, 0.0) - sp
    log_1mb = -jnp.maximum(z, 0.0) - sp
    hi, lo = _split_bf16(log_1mb)
    r = _dot(jnp.concatenate([hi, lo], axis=0), scan_scr[...])
    r = r[:SUB_ROWS, :] + r[SUB_ROWS:, :]
    run = run_scr[...]
    a = jnp.exp(log_b + r[:, :WINDOW] + jnp.concatenate([run] * (WINDOW // LANES), axis=1))
    run_scr[...] = run + r[:, WINDOW:]

    a_all = jnp.concatenate([jnp.where(head_row == h, a, 0.0).astype(BF16) for h in range(HEADS)], axis=1)
    v_all = jnp.concatenate([head_window(v_refs, h) for h in range(HEADS)], axis=0)
    acc = acc_scr[...] + _dot(a_all, v_all)
    acc_scr[...] = acc

    @pl.when(s == n_steps - 1)
    def _():
        o_ref[...] = acc[0:HEADS, :]


def sb_decode(q_rows, bias_col, cache_k, cache_v, page_table, layer):
    nb, n_pages = page_table.shape
    n_steps = n_pages // PAGES_PER_STEP

    def page_spec(g):
        return pl.BlockSpec(
            (None, None, PAGE_SIZE * HEADS, HEAD_DIM),
            lambda b, s, pt: (layer, pt[b, n_pages - (s + 1) * PAGES_PER_STEP + g], 0, 0))

    pages = [page_spec(g) for g in range(PAGES_PER_STEP)]
    return pl.pallas_call(
        functools.partial(_sb_decode_kernel, n_steps=n_steps),
        grid_spec=pltpu.PrefetchScalarGridSpec(
            num_scalar_prefetch=1,
            grid=(nb, n_steps),
            in_specs=[pl.BlockSpec((None, SUB_ROWS, HEAD_DIM), lambda b, s, pt: (b, 0, 0)),
                      pl.BlockSpec((None, SUB_ROWS, 1), lambda b, s, pt: (layer, 0, 0))] + pages + pages,
            out_specs=pl.BlockSpec((None, HEADS, HEAD_DIM), lambda b, s, pt: (b, 0, 0)),
            scratch_shapes=[pltpu.VMEM((SUB_ROWS, HEADS * HEAD_DIM), BF16),
                            pltpu.VMEM((WINDOW, WINDOW + LANES), BF16),
                            pltpu.VMEM((SUB_ROWS, HEAD_DIM), F32),
                            pltpu.VMEM((SUB_ROWS, LANES), F32)]),
        out_shape=jax.ShapeDtypeStruct((nb, HEADS, HEAD_DIM), F32),
        compiler_params=_params("parallel", "arbitrary"),
        name="sb_decode",
    )(page_table, q_rows, bias_col, *([cache_k] * PAGES_PER_STEP), *([cache_v] * PAGES_PER_STEP))


def _rope_tables(pos):
    half = HEAD_DIM // 2
    inv = ROPE_THETA ** (-jnp.arange(half, dtype=F32) / half)
    ang = pos.astype(F32)[:, None] * inv[None, :]
    cos, sin = jnp.cos(ang), jnp.sin(ang)
    return jnp.concatenate([cos, cos], axis=-1), jnp.concatenate([-sin, sin], axis=-1)


def kernel(x_prompt, x_sample, cache_sb_k, cache_sb_v, state_ret, state_conv, page_table, p_prompt, p_sample,
           ffn1_norm, ffn1_w_gu, ffn1_w_down, mix_norm, w_in, ret_gn, conv_w, sb_q_norm, sb_k_norm, sb_bias,
           w_branch_ret, w_branch_conv, w_branch_sb, w_out, ffn2_norm, ffn2_w_gu, ffn2_w_down,
           ple_norm, w_ple_gate, w_ple_up):
    batch, seq, _ = x_prompt.shape
    nb = x_sample.shape[0]
    m_p = batch * seq
    m_s = 16

    lg =jnp.log1p(-jnp.exp2(-5.0 - jnp.arange(HEADS, dtype=F32)))
    cos_p, sin_p = _rope_tables(jnp.arange(seq, dtype=jnp.int32))
    past_len = page_table.shape[1] * PAGE_SIZE
    cos_s, sin_s = _rope_tables(jnp.full((1,), past_len, jnp.int32))
    tp = lambda t: min(t, m_p)
    page_rows = cache_sb_k.shape[:2] + (PAGE_SIZE * HEADS, HEAD_DIM)
    cache_sb_k, cache_sb_v = cache_sb_k.reshape(page_rows), cache_sb_v.reshape(page_rows)
    row_pad = ((0, 0), (0, SUB_ROWS - HEADS), (0, 0))
    bias_cols = jnp.pad(sb_bias.reshape(DEPTH, HEADS, 1), row_pad)

    xp = x_prompt.reshape(m_p, D_MODEL)
    xs = jnp.pad(x_sample.reshape(nb, D_MODEL), ((0, m_s - nb), (0, 0)))
    pp = p_prompt.reshape(DEPTH, m_p, P_DIM)
    ps = jnp.pad(p_sample.reshape(DEPTH, nb, P_DIM), ((0, 0), (0, m_s - nb), (0, 0)))

    def ffn(xp, xs, norm, w_gu, w_down, i):
        hp, hs = rmsnorm_bf16(xp, xs, norm, i, tp(256))
        ap, a_s = dense("swiglu_mm", _swiglu_body, [(hp, None)], [(hs, None)],
                        [(w_gu, 0), (w_gu, D_FF)], i, D_FF, BF16)
        return dense("ffn_down", functools.partial(_residual_body, scale=0.5),
                     [(ap, None), (xp, 0)], [(a_s, None), (xs, 0)], [(w_down, 0)], i, D_MODEL, F32)

    outs = [[] for _ in range(8)]
    for i in range(DEPTH):
        xp, xs = ffn(xp, xs, ffn1_norm, ffn1_w_gu, ffn1_w_down, i)
        hp, hs = rmsnorm_bf16(xp, xs, mix_norm, i, tp(256))
        proj, proj_s = dense("in_proj", _proj_body, [(hp, None)], [(hs, None)], [(w_in, 0)], i, N_IN, F32)

        ya, s_p = retention_prompt(proj, cos_p, sin_p, lg, ret_gn, i, batch, seq)
        yb, c_p = conv_prompt(proj, conv_w, i, batch, seq)
        sk_p = head_norm(proj, sb_k_norm, COL_SK, i, tp(512))
        yc = sb_prompt(proj, sk_p, sb_q_norm, sb_bias[i], i, batch, seq)
        sv_p = proj[:, COL_SV * MIX_W:(COL_SV + 1) * MIX_W]

        ya_s, yb_s, sq_s, sk_s, s_s, c_s = decode_mix(
            proj_s[:nb].reshape(nb, 1, N_IN), state_ret, state_conv, cos_s, sin_s, lg, ret_gn, conv_w,
            sb_q_norm, sb_k_norm, i, nb)
        q_rows = jnp.pad(sq_s.reshape(nb, HEADS, HEAD_DIM), row_pad)
        yc_s = sb_decode(q_rows, bias_cols, cache_sb_k, cache_sb_v, page_table, i)
        yc_s = yc_s.reshape(nb, MIX_W).astype(BF16)
        pad_rows = lambda y: jnp.pad(y.reshape(nb, MIX_W), ((0, m_s - nb), (0, 0)))
        sv_s = proj_s[:nb, COL_SV * MIX_W:(COL_SV + 1) * MIX_W]

        def branch_rows(ya, yb, yc, proj):
            return [(ya, None), (yb, None), (yc, None),
                    (proj, COL_GA * MIX_W), (proj, COL_GB * MIX_W), (proj, COL_GC * MIX_W)]

        mp, ms = dense("merge_branches", _merge_body, branch_rows(ya, yb, yc, proj),
                       branch_rows(pad_rows(ya_s), pad_rows(yb_s), pad_rows(yc_s), proj_s),
                       [(w_branch_ret, 0), (w_branch_conv, 0), (w_branch_sb, 0)], i, D_MODEL, BF16)
        xp, xs = dense("out_proj", functools.partial(_residual_body, scale=1.0),
                       [(mp, None), (xp, 0)], [(ms, None), (xs, 0)], [(w_out, 0)], i, D_MODEL, F32)
        xp, xs = ffn(xp, xs, ffn2_norm, ffn2_w_gu, ffn2_w_down, i)
        hp, hs = rmsnorm_bf16(xp, xs, ple_norm, i, tp(256))
        xp, xs = dense("ple_update", _ple_body, [(hp, None), (pp[i], None), (xp, 0)],
                       [(hs, None), (ps[i], None), (xs, 0)], [(w_ple_gate, 0), (w_ple_up, 0)],
                       i, D_MODEL, F32)

        for lst, val in zip(outs, (sk_p.reshape(batch, seq, HEADS, HEAD_DIM),
                                   sv_p.reshape(batch, seq, HEADS, HEAD_DIM),
                                   sk_s.reshape(nb, 1, HEADS, HEAD_DIM),
                                   sv_s.reshape(nb, 1, HEADS, HEAD_DIM),
                                   s_p, s_s, c_p, c_s)):
            lst.append(val)

    return (xp.reshape(batch, seq, D_MODEL), xs[:nb].reshape(nb, 1, D_MODEL)) + tuple(jnp.stack(o) for o in outs)
```

```python
import functools

import jax
import jax.numpy as jnp
from jax import lax
from jax.experimental import pallas as pl
from jax.experimental.pallas import tpu as pltpu

F32 = jnp.float32
BF16 = jnp.bfloat16

D_MODEL = 2048
DEPTH = 2
PAGE_SIZE = 128
HEADS = 8
HEAD_DIM = 128
MIX_W = HEADS * HEAD_DIM
CHUNK = 128
D_FF = 5632
P_DIM = 256
ROPE_THETA = 10000.0
EPS = 1e-6
N_IN = 9 * MIX_W + 3 * D_MODEL
COL_RQ, COL_RK, COL_RV, COL_CH, COL_CB, COL_CC, COL_SQ, COL_SK, COL_SV = range(9)
COL_GA, COL_GB, COL_GC = 9, 11, 13

VMEM_LIMIT_BYTES = 56 * 1024 * 1024


def _params(*sem):
    return pltpu.CompilerParams(dimension_semantics=sem, vmem_limit_bytes=VMEM_LIMIT_BYTES)


def _dot(a, b):
    return jnp.dot(a, b, preferred_element_type=F32)


def _dot_nt(a, b):
    return lax.dot_general(a, b, (((1,), (1,)), ((), ())), preferred_element_type=F32)


def _dot_tn(a, b):
    return lax.dot_general(a, b, (((0,), (0,)), ((), ())), preferred_element_type=F32)


def _rms(x, g):
    return x * lax.rsqrt(jnp.mean(x * x, axis=-1, keepdims=True) + EPS) * g


def _sigmoid(x):
    return jax.nn.sigmoid(x)


def _row_scale(sumsq_parts):
    return lax.rsqrt(jnp.sum(sumsq_parts, axis=0) * (1.0 / D_MODEL) + EPS)


def _scaled(acc, r):
    return acc * jnp.concatenate([r] * (acc.shape[1] // LANES), axis=1)


def _gain_and_sumsq(x, gain, gained_ref, sumsq_ref):
    gained_ref[...] = (x * gain).astype(gained_ref.dtype)
    sumsq_ref[...] = jnp.broadcast_to(jnp.sum(x * x, axis=1, keepdims=True), sumsq_ref.shape)


def _prenorm_kernel(xp_ref, xs_ref, g_ref, gp_ref, gs_ref, qp_ref, qs_ref):
    _gain_and_sumsq(xp_ref[...], g_ref[...], gp_ref, qp_ref)

    @pl.when(pl.program_id(0) == 0)
    def _():
        _gain_and_sumsq(xs_ref[...], g_ref[...], gs_ref, qs_ref)


def prenorm(xp, xs, g, layer, tm):
    m, d = xp.shape
    ms = xs.shape[0]
    return pl.pallas_call(
        _prenorm_kernel,
        grid=(m // tm,),
        in_specs=[pl.BlockSpec((tm, d), lambda i: (i, 0)),
                  pl.BlockSpec((ms, d), lambda i: (0, 0)),
                  pl.BlockSpec((None, 1, d), lambda i: (layer, 0, 0))],
        out_specs=[pl.BlockSpec((tm, d), lambda i: (i, 0)),
                   pl.BlockSpec((ms, d), lambda i: (0, 0)),
                   pl.BlockSpec((None, tm, LANES), lambda i: (0, i, 0)),
                   pl.BlockSpec((None, ms, LANES), lambda i: (0, 0, 0))],
        out_shape=[jax.ShapeDtypeStruct((m, d), BF16), jax.ShapeDtypeStruct((ms, d), BF16),
                   jax.ShapeDtypeStruct((1, m, LANES), F32), jax.ShapeDtypeStruct((1, ms, LANES), F32)],
        compiler_params=_params("arbitrary"),
        name="prenorm",
    )(xp, xs, g.reshape(DEPTH, 1, d))


def _dense_kernel(*refs, body, n_rows, n_w, norm_in, norm_out):
    refs = iter(refs)
    take = lambda n: [next(refs) for _ in range(n)]
    rows_p, rows_s = take(n_rows), take(n_rows)
    sumsq_in = take(2) if norm_in else None
    first_refs, next_refs = take(n_w), take(n_w)
    gain_ref = take(1)[0] if norm_out else None
    op_ref, os_ref = take(2)
    norm_refs = take(4) if norm_out else None
    w_scr = take(n_w)
    scale_scr = take(2) if norm_in else None
    j, i = pl.program_id(0), pl.program_id(1)
    cur = j % 2

    @pl.when((j == 0) & (i == 0))
    def _():
        for first, scr in zip(first_refs, w_scr):
            scr[0] = first[...].astype(BF16)
        if norm_in:
            scale_scr[1][...] = _row_scale(sumsq_in[1][...])

    for nxt, scr in zip(next_refs, w_scr):
        rows = nxt.shape[0]
        scr[1 - cur, pl.ds(pl.multiple_of(i * rows, rows), rows), :] = nxt[...].astype(BF16)

    if norm_in:
        @pl.when(j == 0)
        def _():
            scale_scr[0][i] = _row_scale(sumsq_in[0][...])

    def emit(rows, scale, o_ref, gained_ref, sumsq_ref):
        out = body([r[...] for r in rows], [scr[cur] for scr in w_scr], scale)
        o_ref[...] = out.astype(o_ref.dtype)
        if norm_out:
            _gain_and_sumsq(out, gain_ref[...], gained_ref, sumsq_ref)

    @pl.when(i == 0)
    def _():
        emit(rows_s, scale_scr[1][...] if norm_in else None, os_ref,
             norm_refs[1] if norm_out else None, norm_refs[3] if norm_out else None)

    emit(rows_p, scale_scr[0][i] if norm_in else None, op_ref,
         norm_refs[0] if norm_out else None, norm_refs[2] if norm_out else None)


DENSE_TILES = {
    "swiglu_mm": (1024, 512), "ffn_down": (512, 512), "in_proj": (1024, 1024),
    "merge_branches": (256, 1024), "out_proj": (512, 1024), "ple_update": (512, 1024),
}


def dense(name, body, rows_p, rows_s, weights, layer, n_out, out_dtype, norm_in=None, norm_out=None):
    tm, tn = DENSE_TILES[name]
    rows_p = [(a, c if c is None else c // tn) for a, c in rows_p]
    rows_s = [(a, c if c is None else c // tn) for a, c in rows_s]
    weights = [(a, c // tn) for a, c in weights]
    m_p, m_s = rows_p[0][0].shape[0], rows_s[0][0].shape[0]
    tm = min(tm, m_p)

    def row_specs(rows, t, tiled):
        specs = []
        for arr, col in rows:
            if col is None:
                specs.append(pl.BlockSpec((t, arr.shape[1]), (lambda j, i: (i, 0)) if tiled else (lambda j, i: (0, 0))))
            elif tiled:
                specs.append(pl.BlockSpec((t, tn), lambda j, i, col=col: (i, col + j)))
            else:
                specs.append(pl.BlockSpec((t, tn), lambda j, i, col=col: (0, col + j)))
        return specs

    n_j, n_i = n_out // tn, m_p // tm
    first_specs = [pl.BlockSpec((None, w.shape[1], tn), lambda j, i, col=col: (layer, 0, col),
                                pipeline_mode=pl.Buffered(1)) for w, col in weights]
    for w, _ in weights:
        assert w.shape[1] % (n_i * SUB_ROWS) == 0, (name, w.shape, n_i)
    next_specs = [pl.BlockSpec((None, w.shape[1] // n_i, tn),
                               lambda j, i, col=col: (layer, i, col + jnp.minimum(j + 1, n_j - 1)))
                  for w, col in weights]
    tile_p = pl.BlockSpec((tm, tn), lambda j, i: (i, j))
    tile_s = pl.BlockSpec((m_s, tn), lambda j, i: (0, j))
    in_specs = row_specs(rows_p, tm, True) + row_specs(rows_s, m_s, False)
    operands = [a for a, _ in rows_p] + [a for a, _ in rows_s]
    out_specs, out_shape = [tile_p, tile_s], [jax.ShapeDtypeStruct((m_p, n_out), out_dtype),
                                              jax.ShapeDtypeStruct((m_s, n_out), out_dtype)]
    scratch = [pltpu.VMEM((2, w.shape[1], tn), BF16) for w, _ in weights]
    if norm_in is not None:
        parts = norm_in[0].shape[0]
        in_specs += [pl.BlockSpec((parts, tm, LANES), lambda j, i: (0, jnp.where(j == 0, i, n_i - 1), 0)),
                     pl.BlockSpec((parts, m_s, LANES), lambda j, i: (0, 0, 0))]
        operands += list(norm_in)
        scratch += [pltpu.VMEM((n_i, tm, LANES), F32), pltpu.VMEM((m_s, LANES), F32)]
    in_specs += first_specs + next_specs
    operands += [w for w, _ in weights] * 2
    if norm_out is not None:
        gain, gain_layer = norm_out
        in_specs.append(pl.BlockSpec((None, 1, tn), lambda j, i: (gain_layer, 0, j)))
        operands.append(gain.reshape(DEPTH, 1, n_out))
        out_specs += [tile_p, tile_s, pl.BlockSpec((None, tm, LANES), lambda j, i: (j, i, 0)),
                      pl.BlockSpec((None, m_s, LANES), lambda j, i: (j, 0, 0))]
        out_shape += [jax.ShapeDtypeStruct((m_p, n_out), BF16), jax.ShapeDtypeStruct((m_s, n_out), BF16),
                      jax.ShapeDtypeStruct((n_j, m_p, LANES), F32), jax.ShapeDtypeStruct((n_j, m_s, LANES), F32)]
    return pl.pallas_call(
        functools.partial(_dense_kernel, body=body, n_rows=len(rows_p), n_w=len(weights),
                          norm_in=norm_in is not None, norm_out=norm_out is not None),
        grid=(n_j, n_i),
        in_specs=in_specs,
        out_specs=out_specs,
        out_shape=out_shape,
        scratch_shapes=scratch,
        compiler_params=_params("arbitrary", "arbitrary"),
        name=name,
    )(*operands)


def _swiglu_body(rows, ws, r):
    (h,), (wg, wu) = rows, ws
    g = _scaled(_dot(h, wg), r)
    return g * _sigmoid(g) * _scaled(_dot(h, wu), r)


def _residual_body(rows, ws, r, *, scale):
    (a, x), (w,) = rows, ws
    return x + scale * _dot(a, w)


def _proj_body(rows, ws, r):
    return _scaled(_dot(rows[0], ws[0]), r)


def _merge_body(rows, ws, r):
    (ya, yb, yc, ga, gb, gc), (wr, wc, wsb) = rows, ws
    return _sigmoid(ga) * _dot(ya, wr) + _sigmoid(gb) * _dot(yb, wc) + _sigmoid(gc) * _dot(yc, wsb)


def _ple_body(rows, ws, r):
    (h, p, x), (wg, wu) = rows, ws
    return x + _sigmoid(_scaled(_dot(h, wg), r)) * _dot(p.astype(BF16), wu)


def _rope(x, cos2, sin2):
    return x * cos2 + pltpu.roll(x, HEAD_DIM // 2, 1) * sin2


def _softplus_neg_abs(z):
    neg_abs = lax.bitcast_convert_type(
        lax.bitcast_convert_type(z, jnp.uint32) | jnp.uint32(0x80000000), F32)
    return jnp.log(1.0 + jnp.exp(neg_abs))


def _split_bf16(x):
    hi = x.astype(BF16)
    return hi, (x - hi.astype(F32)).astype(BF16)


def _retention_kernel(lg_ref, q_ref, k_ref, v_ref, cos_ref, sin_ref, gn_ref, ya_ref, s_out_ref, s_scr,
                      *, n_chunks):
    c = pl.program_id(1)

    @pl.when(c == 0)
    def _():
        s_scr[...] = jnp.zeros_like(s_scr)

    cos2 = cos_ref[...]
    sin2 = sin_ref[...]
    ii = lax.broadcasted_iota(jnp.int32, (CHUNK, CHUNK), 0)
    jj = lax.broadcasted_iota(jnp.int32, (CHUNK, CHUNK), 1)
    diff = (ii - jj).astype(F32)
    pos = lax.broadcasted_iota(jnp.int32, (CHUNK, 1), 0).astype(F32)
    for h in range(HEADS):
        cols = slice(h * HEAD_DIM, (h + 1) * HEAD_DIM)
        lg = lg_ref[h]
        q = _rope(q_ref[:, cols], cos2, sin2)
        k = _rope(k_ref[:, cols], cos2, sin2) * (HEAD_DIM ** -0.5)
        v = v_ref[:, cols].astype(BF16)
        qb = q.astype(BF16)
        dmask = jnp.where(diff >= 0, jnp.exp(lg * jnp.maximum(diff, 0.0)), 0.0)
        scores = _dot_nt(qb, k.astype(BF16)) * dmask
        inner = _dot(scores.astype(BF16), v)
        s = s_scr[h]
        cross = _dot(qb, s.astype(BF16)) * jnp.exp(lg * (pos + 1.0))
        kd = (k * jnp.exp(lg * (CHUNK - 1.0 - pos))).astype(BF16)
        s_scr[h] = jnp.exp(lg * CHUNK) * s + _dot_tn(kd, v)
        o = inner + cross
        ya_ref[:, cols] = _rms(o, gn_ref[h:h + 1, :]).astype(ya_ref.dtype)

    @pl.when(c == n_chunks - 1)
    def _():
        s_out_ref[...] = s_scr[...]


def retention_prompt(proj, cos2, sin2, lg, gn, layer, batch, seq):
    n_chunks = seq // CHUNK
    m = batch * seq

    def col_spec(col):
        return pl.BlockSpec((CHUNK, MIX_W), lambda b, c, lg_ref: (b * n_chunks + c, col))

    tab_spec = pl.BlockSpec((CHUNK, HEAD_DIM), lambda b, c, lg_ref: (c, 0))
    return pl.pallas_call(
        functools.partial(_retention_kernel, n_chunks=n_chunks),
        grid_spec=pltpu.PrefetchScalarGridSpec(
            num_scalar_prefetch=1,
            grid=(batch, n_chunks),
            in_specs=[col_spec(COL_RQ), col_spec(COL_RK), col_spec(COL_RV), tab_spec, tab_spec,
                      pl.BlockSpec((None, HEADS, HEAD_DIM), lambda b, c, lg_ref: (layer, 0, 0))],
            out_specs=[pl.BlockSpec((CHUNK, MIX_W), lambda b, c, lg_ref: (b * n_chunks + c, 0)),
                       pl.BlockSpec((None, HEADS, HEAD_DIM, HEAD_DIM), lambda b, c, lg_ref: (b, 0, 0, 0))],
            scratch_shapes=[pltpu.VMEM((HEADS, HEAD_DIM, HEAD_DIM), F32)]),
        out_shape=[jax.ShapeDtypeStruct((m, MIX_W), BF16),
                   jax.ShapeDtypeStruct((batch, HEADS, HEAD_DIM, HEAD_DIM), F32)],
        compiler_params=_params("parallel", "arbitrary"),
        name="retention_prompt",
    )(lg, proj, proj, proj, cos2, sin2, gn)


def _conv_kernel(ch_ref, cb_ref, cc_ref, w_ref, yb_ref, cn_ref):
    u = cc_ref[...] * ch_ref[...]
    seq = u.shape[0]
    t = lax.broadcasted_iota(jnp.int32, u.shape, 0)
    u1 = jnp.where(t >= 1, pltpu.roll(u, 1, 0), 0.0)
    u2 = jnp.where(t >= 2, pltpu.roll(u, 2, 0), 0.0)
    cv = u2 * w_ref[0:1, :] + u1 * w_ref[1:2, :] + u * w_ref[2:3, :]
    yb_ref[...] = (cb_ref[...] * cv).astype(yb_ref.dtype)
    cn_ref[...] = cc_ref[seq - 2:seq, :] * ch_ref[seq - 2:seq, :]


def conv_prompt(proj, conv_w, layer, batch, seq):
    tw = 256
    nw = MIX_W // tw

    def col_spec(col):
        return pl.BlockSpec((seq, tw), lambda b, j: (b, col * nw + j))

    return pl.pallas_call(
        _conv_kernel,
        grid=(batch, nw),
        in_specs=[col_spec(COL_CH), col_spec(COL_CB), col_spec(COL_CC),
                  pl.BlockSpec((None, 3, tw), lambda b, j: (layer, 0, j))],
        out_specs=[pl.BlockSpec((seq, tw), lambda b, j: (b, j)),
                   pl.BlockSpec((None, 2, tw), lambda b, j: (b, 0, j))],
        out_shape=[jax.ShapeDtypeStruct((batch * seq, MIX_W), BF16),
                   jax.ShapeDtypeStruct((batch, 2, MIX_W), F32)],
        compiler_params=_params("parallel", "parallel"),
        name="conv_prompt",
    )(proj, proj, proj, conv_w)


def _new_kv_kernel(k_ref, v_ref, g_ref, *refs, layer, first):
    ok_ref, ov_ref = refs[-2:]

    def write_layer():
        g = g_ref[...]
        for h in range(HEADS):
            cols = slice(h * HEAD_DIM, (h + 1) * HEAD_DIM)
            ok_ref[:, cols] = _rms(k_ref[:, cols], g)
        ov_ref[...] = v_ref[...]

    if first:
        pl.when(pl.program_id(0) == layer)(write_layer)

        @pl.when(pl.program_id(0) != layer)
        def _():
            ok_ref[...] = jnp.zeros_like(ok_ref)
            ov_ref[...] = jnp.zeros_like(ov_ref)
    else:
        write_layer()


def new_kv(proj, g, layer, prev, tm):
    m = proj.shape[0]
    n_i = m // tm
    first = prev is None
    if first:
        grid = (DEPTH, n_i)
        row = lambda l, i: jnp.where(l == layer, i, 0)
        in_specs = [pl.BlockSpec((tm, MIX_W), lambda l, i: (row(l, i), COL_SK)),
                    pl.BlockSpec((tm, MIX_W), lambda l, i: (row(l, i), COL_SV)),
                    pl.BlockSpec((None, 1, HEAD_DIM), lambda l, i: (layer, 0, 0))]
        out_spec = pl.BlockSpec((None, tm, MIX_W), lambda l, i: (l, i, 0))
        operands, aliases, sem = [proj, proj, g.reshape(DEPTH, 1, HEAD_DIM)], {}, ("arbitrary", "arbitrary")
    else:
        grid = (n_i,)
        in_specs = [pl.BlockSpec((tm, MIX_W), lambda i: (i, COL_SK)),
                    pl.BlockSpec((tm, MIX_W), lambda i: (i, COL_SV)),
                    pl.BlockSpec((None, 1, HEAD_DIM), lambda i: (layer, 0, 0)),
                    pl.BlockSpec(memory_space=pl.ANY), pl.BlockSpec(memory_space=pl.ANY)]
        out_spec = pl.BlockSpec((None, tm, MIX_W), lambda i: (layer, i, 0))
        operands = [proj, proj, g.reshape(DEPTH, 1, HEAD_DIM), prev[0], prev[1]]
        aliases, sem = {3: 0, 4: 1}, ("arbitrary",)
    return pl.pallas_call(
        functools.partial(_new_kv_kernel, layer=layer, first=first),
        grid=grid,
        in_specs=in_specs,
        out_specs=[out_spec, out_spec],
        out_shape=[jax.ShapeDtypeStruct((DEPTH, m, MIX_W), F32)] * 2,
        input_output_aliases=aliases,
        compiler_params=_params(*sem),
        name="new_kv",
    )(*operands)


SB_TILE = 256


SB_HEADS_PER_STEP = 8


def _sb_prompt_kernel(bias_ref, q_ref, k_ref, v_ref, qg_ref, o_ref, acc_scr, run_scr):
    t = SB_TILE
    w = HEAD_DIM
    hg = pl.program_id(1)
    qi = pl.program_id(2)
    row = lax.broadcasted_iota(jnp.int32, (t, t), 0)
    col = lax.broadcasted_iota(jnp.int32, (t, t), 1)
    causal = col < row
    key = jnp.bitwise_and(lax.broadcasted_iota(jnp.int32, (2 * w, 2 * w), 0), w - 1)
    c2 = lax.broadcasted_iota(jnp.int32, (2 * w, 2 * w), 1)
    scan = jnp.where((key > c2) | (c2 >= w), 1.0, 0.0).astype(BF16)
    acc_scr[...] = jnp.zeros_like(acc_scr)
    run_scr[...] = jnp.zeros_like(run_scr)
    heads = range(SB_HEADS_PER_STEP)
    cols = [slice(hh * HEAD_DIM, (hh + 1) * HEAD_DIM) for hh in heads]
    qn = [_rms(q_ref[:, cols[hh]], qg_ref[...]).astype(BF16) for hh in heads]
    bias = [bias_ref[hg * SB_HEADS_PER_STEP + hh] for hh in heads]

    def block(kj, diagonal):
        start = pl.multiple_of(kj * t, t)
        scores = [_dot_nt(qn[hh], k_ref[pl.ds(start, t), cols[hh]].astype(BF16)) for hh in heads]
        zs, us, scanned = [], [], []
        for hh in heads:
            z = scores[hh] * (HEAD_DIM ** -0.5) + bias[hh]
            u = jnp.maximum(z, 0.0) + _softplus_neg_abs(z)
            if diagonal:
                u = jnp.where(causal, u, 0.0)
            halves = []
            for part in (u[:, :w], u[:, w:]):
                hi, lo = _split_bf16(part)
                halves.append(_dot(jnp.concatenate([hi, lo], axis=1), scan))
            zs.append(z)
            us.append(u)
            scanned.append(halves)
        weights = []
        for hh in heads:
            (left, right), run = scanned[hh], run_scr[hh]
            right_of_left = right[:, w:] + run
            s_left = us[hh][:, :w] + left[:, :w] + right_of_left
            s_right = us[hh][:, w:] + right[:, :w] + run
            e = jnp.exp(zs[hh] - jnp.concatenate([s_left, s_right], axis=1))
            if diagonal:
                e = jnp.where(causal, e, 0.0)
            weights.append(e.astype(BF16))
            run_scr[hh] = left[:, w:] + right_of_left
        for hh in heads:
            acc_scr[hh] += _dot(weights[hh], v_ref[pl.ds(start, t), cols[hh]].astype(BF16))

    block(qi, True)

    def body(i, carry):
        block(qi - 1 - i, False)
        return carry

    lax.fori_loop(0, qi, body, 0)
    for hh in heads:
        o_ref[:, cols[hh]] = acc_scr[hh].astype(o_ref.dtype)


def sb_prompt(proj, sk, q_gain, bias, layer, batch, seq):
    t = SB_TILE
    nq = seq // t
    hs = SB_HEADS_PER_STEP
    width = hs * HEAD_DIM
    groups = HEADS // hs
    return pl.pallas_call(
        _sb_prompt_kernel,
        grid_spec=pltpu.PrefetchScalarGridSpec(
            num_scalar_prefetch=1,
            grid=(batch, groups, nq),
            in_specs=[pl.BlockSpec((t, width), lambda b, g, qi, s: (b * nq + qi, COL_SQ * groups + g)),
                      pl.BlockSpec((None, seq, width), lambda b, g, qi, s: (layer, b, g)),
                      pl.BlockSpec((seq, width), lambda b, g, qi, s: (b, COL_SV * groups + g)),
                      pl.BlockSpec((None, 1, HEAD_DIM), lambda b, g, qi, s: (layer, 0, 0))],
            out_specs=pl.BlockSpec((t, width), lambda b, g, qi, s: (b * nq + qi, g)),
            scratch_shapes=[pltpu.VMEM((hs, t, HEAD_DIM), F32), pltpu.VMEM((hs, t, HEAD_DIM), F32)]),
        out_shape=jax.ShapeDtypeStruct((batch * seq, MIX_W), BF16),
        compiler_params=_params("parallel", "parallel", "parallel"),
        name="sb_prompt",
    )(bias, proj, sk, proj, q_gain.reshape(DEPTH, 1, HEAD_DIM))


def _decode_mix_kernel(lg_ref, p_ref, s_ref, cprev_ref, cos_ref, sin_ref, gn_ref, cw_ref, qg_ref, kg_ref,
                       ya_ref, yb_ref, sq_ref, sk_ref, s_out_ref, cn_ref):
    cos2 = cos_ref[...]
    sin2 = sin_ref[...]
    row = lax.broadcasted_iota(jnp.int32, (HEAD_DIM, HEAD_DIM), 0)
    col = lax.broadcasted_iota(jnp.int32, (HEAD_DIM, HEAD_DIM), 1)

    def stream(c, h):
        return p_ref[:, c * MIX_W + h * HEAD_DIM:c * MIX_W + (h + 1) * HEAD_DIM]

    def bf_round(x):
        return x.astype(BF16).astype(F32)

    for h in range(HEADS):
        cols = slice(h * HEAD_DIM, (h + 1) * HEAD_DIM)
        lg = lg_ref[h]
        q = bf_round(_rope(stream(COL_RQ, h), cos2, sin2))
        k = bf_round(_rope(stream(COL_RK, h), cos2, sin2) * (HEAD_DIM ** -0.5))
        v = bf_round(stream(COL_RV, h))
        s = s_ref[h]
        score = bf_round(jnp.sum(q * k, axis=-1, keepdims=True))
        inner = score * v
        q_rows = jnp.broadcast_to(q, (16, HEAD_DIM)).astype(BF16)
        cross = _dot(q_rows, s.astype(BF16))[0:1, :] * jnp.exp(lg * jnp.ones((1, 1), F32))
        k_col = jnp.sum(jnp.where(row == col, jnp.broadcast_to(k, (HEAD_DIM, HEAD_DIM)), 0.0),
                        axis=1, keepdims=True)
        s_out_ref[h] = jnp.exp(lg * jnp.ones((1, 1), F32)) * s + k_col * v
        ya_ref[:, cols] = _rms(inner + cross, gn_ref[h:h + 1, :]).astype(ya_ref.dtype)
        sq_ref[:, cols] = _rms(stream(COL_SQ, h), qg_ref[...])
        sk_ref[:, cols] = _rms(stream(COL_SK, h), kg_ref[...])

    ch = p_ref[:, COL_CH * MIX_W:(COL_CH + 1) * MIX_W]
    cb = p_ref[:, COL_CB * MIX_W:(COL_CB + 1) * MIX_W]
    cc = p_ref[:, COL_CC * MIX_W:(COL_CC + 1) * MIX_W]
    u = cc * ch
    cv = cprev_ref[0:1, :] * cw_ref[0:1, :] + cprev_ref[1:2, :] * cw_ref[1:2, :] + u * cw_ref[2:3, :]
    yb_ref[...] = (cb * cv).astype(yb_ref.dtype)
    cn_ref[0:1, :] = cprev_ref[1:2, :]
    cn_ref[1:2, :] = u


def decode_mix(proj_s, state_ret, state_conv, cos2, sin2, lg, gn, conv_w, q_gain, k_gain, layer, nb):
    def row_spec(width):
        return pl.BlockSpec((None, 1, width), lambda b, s: (b, 0, 0))

    def w_spec(*shape):
        return pl.BlockSpec((None,) + shape, lambda b, s: (layer,) + (0,) * len(shape))

    tab_spec = pl.BlockSpec((1, HEAD_DIM), lambda b, s: (0, 0))
    return pl.pallas_call(
        _decode_mix_kernel,
        grid_spec=pltpu.PrefetchScalarGridSpec(
            num_scalar_prefetch=1,
            grid=(nb,),
            in_specs=[row_spec(N_IN),
                      pl.BlockSpec((None, None, HEADS, HEAD_DIM, HEAD_DIM), lambda b, s: (layer, b, 0, 0, 0)),
                      pl.BlockSpec((None, None, 2, MIX_W), lambda b, s: (layer, b, 0, 0)),
                      tab_spec, tab_spec, w_spec(HEADS, HEAD_DIM), w_spec(3, MIX_W),
                      w_spec(1, HEAD_DIM), w_spec(1, HEAD_DIM)],
            out_specs=[row_spec(MIX_W), row_spec(MIX_W), row_spec(MIX_W), row_spec(MIX_W),
                       pl.BlockSpec((None, HEADS, HEAD_DIM, HEAD_DIM), lambda b, s: (b, 0, 0, 0)),
                       pl.BlockSpec((None, 2, MIX_W), lambda b, s: (b, 0, 0))]),
        out_shape=[jax.ShapeDtypeStruct((nb, 1, MIX_W), BF16),
                   jax.ShapeDtypeStruct((nb, 1, MIX_W), BF16),
                   jax.ShapeDtypeStruct((nb, 1, MIX_W), F32),
                   jax.ShapeDtypeStruct((nb, 1, MIX_W), F32),
                   jax.ShapeDtypeStruct((nb, HEADS, HEAD_DIM, HEAD_DIM), F32),
                   jax.ShapeDtypeStruct((nb, 2, MIX_W), F32)],
        compiler_params=_params("parallel"),
        name="decode_mix",
    )(lg, proj_s, state_ret, state_conv, cos2, sin2, gn, conv_w,
      q_gain.reshape(DEPTH, 1, HEAD_DIM), k_gain.reshape(DEPTH, 1, HEAD_DIM))


PAGES_PER_STEP = 8
WINDOW = PAGES_PER_STEP * PAGE_SIZE
LANES = 128
SUB_ROWS = 16


def _sb_decode_kernel(pt_ref, q_ref, bias_ref, *refs, n_steps):
    g_n = PAGES_PER_STEP
    k_refs, v_refs = refs[:g_n], refs[g_n:2 * g_n]
    o_ref, qb_scr, scan_scr, acc_scr, run_scr = refs[2 * g_n:]
    s = pl.program_id(1)
    head_row = lax.broadcasted_iota(jnp.int32, (SUB_ROWS, WINDOW), 0)

    def head_window(page_refs, h):
        return jnp.concatenate(
            [r[pl.ds(h, PAGE_SIZE, stride=HEADS), :].astype(BF16) for r in page_refs], axis=0)

    @pl.when(s == 0)
    def _():
        acc_scr[...] = jnp.zeros_like(acc_scr)
        run_scr[...] = jnp.zeros_like(run_scr)
        q = q_ref[...]
        for h in range(HEADS):
            qb_scr[:, h * HEAD_DIM:(h + 1) * HEAD_DIM] = jnp.where(head_row[:, :HEAD_DIM] == h, q, 0.0).astype(BF16)
        r2 = lax.broadcasted_iota(jnp.int32, (WINDOW, WINDOW + LANES), 0)
        c2 = lax.broadcasted_iota(jnp.int32, (WINDOW, WINDOW + LANES), 1)
        scan_scr[...] = jnp.where((r2 > c2) | (c2 >= WINDOW), 1.0, 0.0).astype(BF16)

    k_all = jnp.concatenate([head_window(k_refs, h) for h in range(HEADS)], axis=1)
    z = _dot_nt(qb_scr[...], k_all) * (HEAD_DIM ** -0.5) + bias_ref[...]
    sp = _softplus_neg_abs(z)
    log_b = jnp.minimum(z, 0.0) - sp
    log_1mb = -jnp.maximum(z, 0.0) - sp
    hi, lo = _split_bf16(log_1mb)
    r = _dot(jnp.concatenate([hi, lo], axis=0), scan_scr[...])
    r = r[:SUB_ROWS, :] + r[SUB_ROWS:, :]
    run = run_scr[...]
    a = jnp.exp(log_b + r[:, :WINDOW] + jnp.concatenate([run] * (WINDOW // LANES), axis=1))
    run_scr[...] = run + r[:, WINDOW:]

    a_all = jnp.concatenate([jnp.where(head_row == h, a, 0.0).astype(BF16) for h in range(HEADS)], axis=1)
    v_all = jnp.concatenate([head_window(v_refs, h) for h in range(HEADS)], axis=0)
    acc = acc_scr[...] + _dot(a_all, v_all)
    acc_scr[...] = acc

    @pl.when(s == n_steps - 1)
    def _():
        o_ref[...] = acc[0:HEADS, :]


def sb_decode(q_rows, bias_col, cache_k, cache_v, page_table, layer):
    nb, n_pages = page_table.shape
    assert n_pages % PAGES_PER_STEP == 0, (n_pages, PAGES_PER_STEP)
    n_steps = n_pages // PAGES_PER_STEP

    def page_spec(g):
        return pl.BlockSpec(
            (None, None, PAGE_SIZE * HEADS, HEAD_DIM),
            lambda b, s, pt: (layer, pt[b, n_pages - (s + 1) * PAGES_PER_STEP + g], 0, 0))

    pages = [page_spec(g) for g in range(PAGES_PER_STEP)]
    return pl.pallas_call(
        functools.partial(_sb_decode_kernel, n_steps=n_steps),
        grid_spec=pltpu.PrefetchScalarGridSpec(
            num_scalar_prefetch=1,
            grid=(nb, n_steps),
            in_specs=[pl.BlockSpec((None, SUB_ROWS, HEAD_DIM), lambda b, s, pt: (b, 0, 0)),
                      pl.BlockSpec((None, SUB_ROWS, 1), lambda b, s, pt: (layer, 0, 0))] + pages + pages,
            out_specs=pl.BlockSpec((None, HEADS, HEAD_DIM), lambda b, s, pt: (b, 0, 0)),
            scratch_shapes=[pltpu.VMEM((SUB_ROWS, HEADS * HEAD_DIM), BF16),
                            pltpu.VMEM((WINDOW, WINDOW + LANES), BF16),
                            pltpu.VMEM((SUB_ROWS, HEAD_DIM), F32),
                            pltpu.VMEM((SUB_ROWS, LANES), F32)]),
        out_shape=jax.ShapeDtypeStruct((nb, HEADS, HEAD_DIM), F32),
        compiler_params=_params("parallel", "arbitrary"),
        name="sb_decode",
    )(page_table, q_rows, bias_col, *([cache_k] * PAGES_PER_STEP), *([cache_v] * PAGES_PER_STEP))


def _rope_tables(pos):
    half = HEAD_DIM // 2
    inv = ROPE_THETA ** (-jnp.arange(half, dtype=F32) / half)
    ang = pos.astype(F32)[:, None] * inv[None, :]
    cos, sin = jnp.cos(ang), jnp.sin(ang)
    return jnp.concatenate([cos, cos], axis=-1), jnp.concatenate([-sin, sin], axis=-1)


def kernel(x_prompt, x_sample, cache_sb_k, cache_sb_v, state_ret, state_conv, page_table, p_prompt, p_sample,
           ffn1_norm, ffn1_w_gu, ffn1_w_down, mix_norm, w_in, ret_gn, conv_w, sb_q_norm, sb_k_norm, sb_bias,
           w_branch_ret, w_branch_conv, w_branch_sb, w_out, ffn2_norm, ffn2_w_gu, ffn2_w_down,
           ple_norm, w_ple_gate, w_ple_up):
    batch, seq, _ = x_prompt.shape
    nb = x_sample.shape[0]
    m_p = batch * seq
    m_s = 16

    lg =jnp.log1p(-jnp.exp2(-5.0 - jnp.arange(HEADS, dtype=F32)))
    cos_p, sin_p = _rope_tables(jnp.arange(seq, dtype=jnp.int32))
    past_len = page_table.shape[1] * PAGE_SIZE
    cos_s, sin_s = _rope_tables(jnp.full((1,), past_len, jnp.int32))
    tp = lambda t: min(t, m_p)
    page_rows = cache_sb_k.shape[:2] + (PAGE_SIZE * HEADS, HEAD_DIM)
    cache_sb_k, cache_sb_v = cache_sb_k.reshape(page_rows), cache_sb_v.reshape(page_rows)
    row_pad = ((0, 0), (0, SUB_ROWS - HEADS), (0, 0))
    bias_cols = jnp.pad(sb_bias.reshape(DEPTH, HEADS, 1), row_pad)

    xp = x_prompt.reshape(m_p, D_MODEL)
    xs = jnp.pad(x_sample.reshape(nb, D_MODEL), ((0, m_s - nb), (0, 0)))
    pp = p_prompt.reshape(DEPTH, m_p, P_DIM)
    ps = jnp.pad(p_sample.reshape(DEPTH, nb, P_DIM), ((0, 0), (0, m_s - nb), (0, 0)))

    def ffn(stream, w_gu, w_down, next_gain, i):
        xp, xs, gp, gs, qp, qs = stream
        ap, a_s = dense("swiglu_mm", _swiglu_body, [(gp, None)], [(gs, None)],
                        [(w_gu, 0), (w_gu, D_FF)], i, D_FF, BF16, norm_in=(qp, qs))
        return dense("ffn_down", functools.partial(_residual_body, scale=0.5),
                     [(ap, None), (xp, 0)], [(a_s, None), (xs, 0)], [(w_down, 0)], i, D_MODEL, F32,
                     norm_out=(next_gain, i))

    stream = (xp, xs) + tuple(prenorm(xp, xs, ffn1_norm, 0, tp(256)))
    outs = [[] for _ in range(6)]
    kv_p = None
    for i in range(DEPTH):
        stream = ffn(stream, ffn1_w_gu, ffn1_w_down, mix_norm, i)
        xp, xs, gp, gs, qp, qs = stream
        proj, proj_s = dense("in_proj", _proj_body, [(gp, None)], [(gs, None)], [(w_in, 0)], i, N_IN, F32,
                             norm_in=(qp, qs))

        ya, s_p = retention_prompt(proj, cos_p, sin_p, lg, ret_gn, i, batch, seq)
        yb, c_p = conv_prompt(proj, conv_w, i, batch, seq)
        kv_p = new_kv(proj, sb_k_norm, i, kv_p, tp(512))
        yc = sb_prompt(proj, kv_p[0], sb_q_norm, sb_bias[i], i, batch, seq)

        ya_s, yb_s, sq_s, sk_s, s_s, c_s = decode_mix(
            proj_s[:nb].reshape(nb, 1, N_IN), state_ret, state_conv, cos_s, sin_s, lg, ret_gn, conv_w,
            sb_q_norm, sb_k_norm, i, nb)
        q_rows = jnp.pad(sq_s.reshape(nb, HEADS, HEAD_DIM), row_pad)
        yc_s = sb_decode(q_rows, bias_cols, cache_sb_k, cache_sb_v, page_table, i)
        yc_s = yc_s.reshape(nb, MIX_W).astype(BF16)
        pad_rows = lambda y: jnp.pad(y.reshape(nb, MIX_W), ((0, m_s - nb), (0, 0)))
        sv_s = proj_s[:nb, COL_SV * MIX_W:(COL_SV + 1) * MIX_W]

        def branch_rows(ya, yb, yc, proj):
            return [(ya, None), (yb, None), (yc, None),
                    (proj, COL_GA * MIX_W), (proj, COL_GB * MIX_W), (proj, COL_GC * MIX_W)]

        mp, ms = dense("merge_branches", _merge_body, branch_rows(ya, yb, yc, proj),
                       branch_rows(pad_rows(ya_s), pad_rows(yb_s), pad_rows(yc_s), proj_s),
                       [(w_branch_ret, 0), (w_branch_conv, 0), (w_branch_sb, 0)], i, D_MODEL, BF16)
        stream = dense("out_proj", functools.partial(_residual_body, scale=1.0),
                       [(mp, None), (xp, 0)], [(ms, None), (xs, 0)], [(w_out, 0)], i, D_MODEL, F32,
                       norm_out=(ffn2_norm, i))
        xp, xs, gp, gs, qp, qs = ffn(stream, ffn2_w_gu, ffn2_w_down, ple_norm, i)
        stream = dense("ple_update", _ple_body, [(gp, None), (pp[i], None), (xp, 0)],
                       [(gs, None), (ps[i], None), (xs, 0)], [(w_ple_gate, 0), (w_ple_up, 0)],
                       i, D_MODEL, F32, norm_in=(qp, qs),
                       norm_out=(ffn1_norm, i + 1) if i + 1 < DEPTH else None)
        xp, xs = stream[:2]

        for lst, val in zip(outs, (sk_s.reshape(nb, 1, HEADS, HEAD_DIM),
                                   sv_s.reshape(nb, 1, HEADS, HEAD_DIM),
                                   s_p, s_s, c_p, c_s)):
            lst.append(val)

    new_k_p, new_v_p = (a.reshape(DEPTH, batch, seq, HEADS, HEAD_DIM) for a in kv_p)
    return ((xp.reshape(batch, seq, D_MODEL), xs[:nb].reshape(nb, 1, D_MODEL), new_k_p, new_v_p)
            + tuple(jnp.stack(o) for o in outs))
```

```python
import functools

import jax
import jax.numpy as jnp
from jax import lax
from jax.experimental import pallas as pl
from jax.experimental.pallas import tpu as pltpu

F32 = jnp.float32
BF16 = jnp.bfloat16

D_MODEL = 2048
DEPTH = 2
PAGE_SIZE = 128
HEADS = 8
HEAD_DIM = 128
MIX_W = HEADS * HEAD_DIM
CHUNK = 128
D_FF = 5632
P_DIM = 256
ROPE_THETA = 10000.0
EPS = 1e-6
N_IN = 9 * MIX_W + 3 * D_MODEL
COL_RQ, COL_RK, COL_RV, COL_CH, COL_CB, COL_CC, COL_SQ, COL_SK, COL_SV = range(9)
COL_GA, COL_GB, COL_GC = 9, 11, 13

VMEM_LIMIT_BYTES = 56 * 1024 * 1024


def _params(*sem):
    return pltpu.CompilerParams(dimension_semantics=sem, vmem_limit_bytes=VMEM_LIMIT_BYTES)


def _dot(a, b):
    return jnp.dot(a, b, preferred_element_type=F32)


def _dot_nt(a, b):
    return lax.dot_general(a, b, (((1,), (1,)), ((), ())), preferred_element_type=F32)


def _dot_tn(a, b):
    return lax.dot_general(a, b, (((0,), (0,)), ((), ())), preferred_element_type=F32)


def _rms(x, g):
    return x * lax.rsqrt(jnp.mean(x * x, axis=-1, keepdims=True) + EPS) * g


def _sigmoid(x):
    return jax.nn.sigmoid(x)


def _row_scale(sumsq_parts):
    return lax.rsqrt(jnp.sum(sumsq_parts, axis=0) * (1.0 / D_MODEL) + EPS)


def _scaled(acc, r):
    return acc * jnp.concatenate([r] * (acc.shape[1] // LANES), axis=1)


def _gain_and_sumsq(x, gain, gained_ref, sumsq_ref):
    gained_ref[...] = (x * gain).astype(gained_ref.dtype)
    sumsq_ref[...] = jnp.broadcast_to(jnp.sum(x * x, axis=1, keepdims=True), sumsq_ref.shape)


def _prenorm_kernel(xp_ref, xs_ref, g_ref, gp_ref, gs_ref, qp_ref, qs_ref):
    _gain_and_sumsq(xp_ref[...], g_ref[...], gp_ref, qp_ref)

    @pl.when(pl.program_id(0) == 0)
    def _():
        _gain_and_sumsq(xs_ref[...], g_ref[...], gs_ref, qs_ref)


def prenorm(xp, xs, g, layer, tm):
    m, d = xp.shape
    ms = xs.shape[0]
    return pl.pallas_call(
        _prenorm_kernel,
        grid=(m // tm,),
        in_specs=[pl.BlockSpec((tm, d), lambda i: (i, 0)),
                  pl.BlockSpec((ms, d), lambda i: (0, 0)),
                  pl.BlockSpec((None, 1, d), lambda i: (layer, 0, 0))],
        out_specs=[pl.BlockSpec((tm, d), lambda i: (i, 0)),
                   pl.BlockSpec((ms, d), lambda i: (0, 0)),
                   pl.BlockSpec((None, tm, LANES), lambda i: (0, i, 0)),
                   pl.BlockSpec((None, ms, LANES), lambda i: (0, 0, 0))],
        out_shape=[jax.ShapeDtypeStruct((m, d), BF16), jax.ShapeDtypeStruct((ms, d), BF16),
                   jax.ShapeDtypeStruct((1, m, LANES), F32), jax.ShapeDtypeStruct((1, ms, LANES), F32)],
        compiler_params=_params("arbitrary"),
        name="prenorm",
    )(xp, xs, g.reshape(DEPTH, 1, d))


def _dense_kernel(*refs, body, n_rows, n_w, norm_in, norm_out):
    refs = iter(refs)
    take = lambda n: [next(refs) for _ in range(n)]
    rows_p, rows_s = take(n_rows), take(n_rows)
    sumsq_in = take(2) if norm_in else None
    first_refs, next_refs = take(n_w), take(n_w)
    gain_ref = take(1)[0] if norm_out else None
    op_ref, os_ref = take(2)
    norm_refs = take(4) if norm_out else None
    w_scr = take(n_w)
    scale_scr = take(2) if norm_in else None
    j, i = pl.program_id(0), pl.program_id(1)
    cur = j % 2

    @pl.when((j == 0) & (i == 0))
    def _():
        for first, scr in zip(first_refs, w_scr):
            scr[0] = first[...].astype(BF16)
        if norm_in:
            scale_scr[1][...] = _row_scale(sumsq_in[1][...])

    for nxt, scr in zip(next_refs, w_scr):
        rows = nxt.shape[0]
        scr[1 - cur, pl.ds(pl.multiple_of(i * rows, rows), rows), :] = nxt[...].astype(BF16)

    if norm_in:
        @pl.when(j == 0)
        def _():
            scale_scr[0][i] = _row_scale(sumsq_in[0][...])

    def emit(rows, scale, o_ref, gained_ref, sumsq_ref):
        out = body([r[...] for r in rows], [scr[cur] for scr in w_scr], scale)
        o_ref[...] = out.astype(o_ref.dtype)
        if norm_out:
            _gain_and_sumsq(out, gain_ref[...], gained_ref, sumsq_ref)

    @pl.when(i == 0)
    def _():
        emit(rows_s, scale_scr[1][...] if norm_in else None, os_ref,
             norm_refs[1] if norm_out else None, norm_refs[3] if norm_out else None)

    emit(rows_p, scale_scr[0][i] if norm_in else None, op_ref,
         norm_refs[0] if norm_out else None, norm_refs[2] if norm_out else None)


DENSE_TILES = {
    "swiglu_mm": (1024, 512), "ffn_down": (512, 512), "in_proj": (1024, 1024),
    "merge_branches": (512, 1024), "out_proj": (512, 1024), "ple_update": (512, 1024),
}


def dense(name, body, rows_p, rows_s, weights, layer, n_out, out_dtype, norm_in=None, norm_out=None):
    tm, tn = DENSE_TILES[name]
    rows_p = [(a, c if c is None else c // tn) for a, c in rows_p]
    rows_s = [(a, c if c is None else c // tn) for a, c in rows_s]
    weights = [(a, c // tn) for a, c in weights]
    m_p, m_s = rows_p[0][0].shape[0], rows_s[0][0].shape[0]
    tm = min(tm, m_p)

    def row_specs(rows, t, tiled):
        specs = []
        for arr, col in rows:
            if col is None:
                specs.append(pl.BlockSpec((t, arr.shape[1]), (lambda j, i: (i, 0)) if tiled else (lambda j, i: (0, 0))))
            elif tiled:
                specs.append(pl.BlockSpec((t, tn), lambda j, i, col=col: (i, col + j)))
            else:
                specs.append(pl.BlockSpec((t, tn), lambda j, i, col=col: (0, col + j)))
        return specs

    n_j, n_i = n_out // tn, m_p // tm
    first_specs = [pl.BlockSpec((None, w.shape[1], tn), lambda j, i, col=col: (layer, 0, col),
                                pipeline_mode=pl.Buffered(1)) for w, col in weights]
    for w, _ in weights:
        assert w.shape[1] % (n_i * SUB_ROWS) == 0, (name, w.shape, n_i)
    next_specs = [pl.BlockSpec((None, w.shape[1] // n_i, tn),
                               lambda j, i, col=col: (layer, i, col + jnp.minimum(j + 1, n_j - 1)))
                  for w, col in weights]
    tile_p = pl.BlockSpec((tm, tn), lambda j, i: (i, j))
    tile_s = pl.BlockSpec((m_s, tn), lambda j, i: (0, j))
    in_specs = row_specs(rows_p, tm, True) + row_specs(rows_s, m_s, False)
    operands = [a for a, _ in rows_p] + [a for a, _ in rows_s]
    out_specs, out_shape = [tile_p, tile_s], [jax.ShapeDtypeStruct((m_p, n_out), out_dtype),
                                              jax.ShapeDtypeStruct((m_s, n_out), out_dtype)]
    scratch = [pltpu.VMEM((2, w.shape[1], tn), BF16) for w, _ in weights]
    if norm_in is not None:
        parts = norm_in[0].shape[0]
        in_specs += [pl.BlockSpec((parts, tm, LANES), lambda j, i: (0, jnp.where(j == 0, i, n_i - 1), 0)),
                     pl.BlockSpec((parts, m_s, LANES), lambda j, i: (0, 0, 0))]
        operands += list(norm_in)
        scratch += [pltpu.VMEM((n_i, tm, LANES), F32), pltpu.VMEM((m_s, LANES), F32)]
    in_specs += first_specs + next_specs
    operands += [w for w, _ in weights] * 2
    if norm_out is not None:
        gain, gain_layer = norm_out
        in_specs.append(pl.BlockSpec((None, 1, tn), lambda j, i: (gain_layer, 0, j)))
        operands.append(gain.reshape(DEPTH, 1, n_out))
        out_specs += [tile_p, tile_s, pl.BlockSpec((None, tm, LANES), lambda j, i: (j, i, 0)),
                      pl.BlockSpec((None, m_s, LANES), lambda j, i: (j, 0, 0))]
        out_shape += [jax.ShapeDtypeStruct((m_p, n_out), BF16), jax.ShapeDtypeStruct((m_s, n_out), BF16),
                      jax.ShapeDtypeStruct((n_j, m_p, LANES), F32), jax.ShapeDtypeStruct((n_j, m_s, LANES), F32)]
    return pl.pallas_call(
        functools.partial(_dense_kernel, body=body, n_rows=len(rows_p), n_w=len(weights),
                          norm_in=norm_in is not None, norm_out=norm_out is not None),
        grid=(n_j, n_i),
        in_specs=in_specs,
        out_specs=out_specs,
        out_shape=out_shape,
        scratch_shapes=scratch,
        compiler_params=_params("arbitrary", "arbitrary"),
        name=name,
    )(*operands)


def _swiglu_body(rows, ws, r):
    (h,), (wg, wu) = rows, ws
    g = _scaled(_dot(h, wg), r)
    return g * _sigmoid(g) * _scaled(_dot(h, wu), r)


def _residual_body(rows, ws, r, *, scale):
    (a, x), (w,) = rows, ws
    return x + scale * _dot(a, w)


def _proj_body(rows, ws, r):
    return _scaled(_dot(rows[0], ws[0]), r)


def _merge_body(rows, ws, r):
    (ya, yb, yc, ga, gb, gc), (wr, wc, wsb) = rows, ws
    return _sigmoid(ga) * _dot(ya, wr) + _sigmoid(gb) * _dot(yb, wc) + _sigmoid(gc) * _dot(yc, wsb)


def _ple_body(rows, ws, r):
    (h, p, x), (wg, wu) = rows, ws
    return x + _sigmoid(_scaled(_dot(h, wg), r)) * _dot(p.astype(BF16), wu)


def _rope(x, cos2, sin2):
    return x * cos2 + pltpu.roll(x, HEAD_DIM // 2, 1) * sin2


def _softplus_neg_abs(z):
    neg_abs = lax.bitcast_convert_type(
        lax.bitcast_convert_type(z, jnp.uint32) | jnp.uint32(0x80000000), F32)
    return jnp.log(1.0 + jnp.exp(neg_abs))


def _split_bf16(x):
    hi = x.astype(BF16)
    return hi, (x - hi.astype(F32)).astype(BF16)


def _retention_kernel(lg_ref, q_ref, k_ref, v_ref, cos_ref, sin_ref, gn_ref, ya_ref, s_out_ref, s_scr,
                      *, n_chunks):
    c = pl.program_id(1)

    @pl.when(c == 0)
    def _():
        s_scr[...] = jnp.zeros_like(s_scr)

    cos2 = cos_ref[...]
    sin2 = sin_ref[...]
    ii = lax.broadcasted_iota(jnp.int32, (CHUNK, CHUNK), 0)
    jj = lax.broadcasted_iota(jnp.int32, (CHUNK, CHUNK), 1)
    diff = (ii - jj).astype(F32)
    pos = lax.broadcasted_iota(jnp.int32, (CHUNK, 1), 0).astype(F32)
    heads = range(HEADS)
    cols = [slice(h * HEAD_DIM, (h + 1) * HEAD_DIM) for h in heads]
    lg = [lg_ref[h] for h in heads]
    qb, kb, kd, vb = [], [], [], []
    for h in heads:
        k = _rope(k_ref[:, cols[h]], cos2, sin2) * (HEAD_DIM ** -0.5)
        qb.append(_rope(q_ref[:, cols[h]], cos2, sin2).astype(BF16))
        kb.append(k.astype(BF16))
        kd.append((k * jnp.exp(lg[h] * (CHUNK - 1.0 - pos))).astype(BF16))
        vb.append(v_ref[:, cols[h]].astype(BF16))
    state = [s_scr[h] for h in heads]
    scores = [_dot_nt(qb[h], kb[h]) for h in heads]
    cross = [_dot(qb[h], state[h].astype(BF16)) for h in heads]
    update = [_dot_tn(kd[h], vb[h]) for h in heads]
    inner = []
    for h in heads:
        dmask = jnp.where(diff >= 0, jnp.exp(lg[h] * jnp.maximum(diff, 0.0)), 0.0)
        inner.append(_dot((scores[h] * dmask).astype(BF16), vb[h]))
    for h in heads:
        o = inner[h] + cross[h] * jnp.exp(lg[h] * (pos + 1.0))
        ya_ref[:, cols[h]] = _rms(o, gn_ref[h:h + 1, :]).astype(ya_ref.dtype)
        s_scr[h] = jnp.exp(lg[h] * CHUNK) * state[h] + update[h]

    @pl.when(c == n_chunks - 1)
    def _():
        s_out_ref[...] = s_scr[...]


def retention_prompt(proj, cos2, sin2, lg, gn, layer, batch, seq):
    n_chunks = seq // CHUNK
    m = batch * seq

    def col_spec(col):
        return pl.BlockSpec((CHUNK, MIX_W), lambda b, c, lg_ref: (b * n_chunks + c, col))

    tab_spec = pl.BlockSpec((CHUNK, HEAD_DIM), lambda b, c, lg_ref: (c, 0))
    return pl.pallas_call(
        functools.partial(_retention_kernel, n_chunks=n_chunks),
        grid_spec=pltpu.PrefetchScalarGridSpec(
            num_scalar_prefetch=1,
            grid=(batch, n_chunks),
            in_specs=[col_spec(COL_RQ), col_spec(COL_RK), col_spec(COL_RV), tab_spec, tab_spec,
                      pl.BlockSpec((None, HEADS, HEAD_DIM), lambda b, c, lg_ref: (layer, 0, 0))],
            out_specs=[pl.BlockSpec((CHUNK, MIX_W), lambda b, c, lg_ref: (b * n_chunks + c, 0)),
                       pl.BlockSpec((None, HEADS, HEAD_DIM, HEAD_DIM), lambda b, c, lg_ref: (b, 0, 0, 0))],
            scratch_shapes=[pltpu.VMEM((HEADS, HEAD_DIM, HEAD_DIM), F32)]),
        out_shape=[jax.ShapeDtypeStruct((m, MIX_W), BF16),
                   jax.ShapeDtypeStruct((batch, HEADS, HEAD_DIM, HEAD_DIM), F32)],
        compiler_params=_params("parallel", "arbitrary"),
        name="retention_prompt",
    )(lg, proj, proj, proj, cos2, sin2, gn)


def _conv_kernel(ch_ref, cb_ref, cc_ref, w_ref, yb_ref, cn_ref):
    u = cc_ref[...] * ch_ref[...]
    seq = u.shape[0]
    t = lax.broadcasted_iota(jnp.int32, u.shape, 0)
    u1 = jnp.where(t >= 1, pltpu.roll(u, 1, 0), 0.0)
    u2 = jnp.where(t >= 2, pltpu.roll(u, 2, 0), 0.0)
    cv = u2 * w_ref[0:1, :] + u1 * w_ref[1:2, :] + u * w_ref[2:3, :]
    yb_ref[...] = (cb_ref[...] * cv).astype(yb_ref.dtype)
    cn_ref[...] = cc_ref[seq - 2:seq, :] * ch_ref[seq - 2:seq, :]


def conv_prompt(proj, conv_w, layer, batch, seq):
    tw = 256
    nw = MIX_W // tw

    def col_spec(col):
        return pl.BlockSpec((seq, tw), lambda b, j: (b, col * nw + j))

    return pl.pallas_call(
        _conv_kernel,
        grid=(batch, nw),
        in_specs=[col_spec(COL_CH), col_spec(COL_CB), col_spec(COL_CC),
                  pl.BlockSpec((None, 3, tw), lambda b, j: (layer, 0, j))],
        out_specs=[pl.BlockSpec((seq, tw), lambda b, j: (b, j)),
                   pl.BlockSpec((None, 2, tw), lambda b, j: (b, 0, j))],
        out_shape=[jax.ShapeDtypeStruct((batch * seq, MIX_W), BF16),
                   jax.ShapeDtypeStruct((batch, 2, MIX_W), F32)],
        compiler_params=_params("parallel", "parallel"),
        name="conv_prompt",
    )(proj, proj, proj, conv_w)


def _new_kv_kernel(k_ref, v_ref, g_ref, *refs, layer, first):
    ok_ref, ov_ref = refs[-2:]

    def write_layer():
        g = g_ref[...]
        for h in range(HEADS):
            cols = slice(h * HEAD_DIM, (h + 1) * HEAD_DIM)
            ok_ref[:, cols] = _rms(k_ref[:, cols], g)
        ov_ref[...] = v_ref[...]

    if first:
        pl.when(pl.program_id(0) == layer)(write_layer)

        @pl.when(pl.program_id(0) != layer)
        def _():
            ok_ref[...] = jnp.zeros_like(ok_ref)
            ov_ref[...] = jnp.zeros_like(ov_ref)
    else:
        write_layer()


def new_kv(proj, g, layer, prev, tm):
    m = proj.shape[0]
    n_i = m // tm
    first = prev is None
    if first:
        grid = (DEPTH, n_i)
        row = lambda l, i: jnp.where(l == layer, i, 0)
        in_specs = [pl.BlockSpec((tm, MIX_W), lambda l, i: (row(l, i), COL_SK)),
                    pl.BlockSpec((tm, MIX_W), lambda l, i: (row(l, i), COL_SV)),
                    pl.BlockSpec((None, 1, HEAD_DIM), lambda l, i: (layer, 0, 0))]
        out_spec = pl.BlockSpec((None, tm, MIX_W), lambda l, i: (l, i, 0))
        operands, aliases, sem = [proj, proj, g.reshape(DEPTH, 1, HEAD_DIM)], {}, ("arbitrary", "arbitrary")
    else:
        grid = (n_i,)
        in_specs = [pl.BlockSpec((tm, MIX_W), lambda i: (i, COL_SK)),
                    pl.BlockSpec((tm, MIX_W), lambda i: (i, COL_SV)),
                    pl.BlockSpec((None, 1, HEAD_DIM), lambda i: (layer, 0, 0)),
                    pl.BlockSpec(memory_space=pl.ANY), pl.BlockSpec(memory_space=pl.ANY)]
        out_spec = pl.BlockSpec((None, tm, MIX_W), lambda i: (layer, i, 0))
        operands = [proj, proj, g.reshape(DEPTH, 1, HEAD_DIM), prev[0], prev[1]]
        aliases, sem = {3: 0, 4: 1}, ("arbitrary",)
    return pl.pallas_call(
        functools.partial(_new_kv_kernel, layer=layer, first=first),
        grid=grid,
        in_specs=in_specs,
        out_specs=[out_spec, out_spec],
        out_shape=[jax.ShapeDtypeStruct((DEPTH, m, MIX_W), F32)] * 2,
        input_output_aliases=aliases,
        compiler_params=_params(*sem),
        name="new_kv",
    )(*operands)


SB_TILE = 256


SB_HEADS_PER_STEP = 8


def _sb_prompt_kernel(bias_ref, q_ref, k_ref, v_ref, qg_ref, o_ref, acc_scr, run_scr):
    t = SB_TILE
    w = HEAD_DIM
    hg = pl.program_id(1)
    qi = pl.program_id(2)
    row = lax.broadcasted_iota(jnp.int32, (t, t), 0)
    col = lax.broadcasted_iota(jnp.int32, (t, t), 1)
    causal = col < row
    key = jnp.bitwise_and(lax.broadcasted_iota(jnp.int32, (2 * w, 2 * w), 0), w - 1)
    c2 = lax.broadcasted_iota(jnp.int32, (2 * w, 2 * w), 1)
    scan = jnp.where((key > c2) | (c2 >= w), 1.0, 0.0).astype(BF16)
    acc_scr[...] = jnp.zeros_like(acc_scr)
    run_scr[...] = jnp.zeros_like(run_scr)
    heads = range(SB_HEADS_PER_STEP)
    cols = [slice(hh * HEAD_DIM, (hh + 1) * HEAD_DIM) for hh in heads]
    qn = [_rms(q_ref[:, cols[hh]], qg_ref[...]).astype(BF16) for hh in heads]
    bias = [bias_ref[hg * SB_HEADS_PER_STEP + hh] for hh in heads]

    def block(kj, diagonal):
        start = pl.multiple_of(kj * t, t)
        scores = [_dot_nt(qn[hh], k_ref[pl.ds(start, t), cols[hh]].astype(BF16)) for hh in heads]
        zs, us, scanned = [], [], []
        for hh in heads:
            z = scores[hh] * (HEAD_DIM ** -0.5) + bias[hh]
            u = jnp.maximum(z, 0.0) + _softplus_neg_abs(z)
            if diagonal:
                u = jnp.where(causal, u, 0.0)
            halves = []
            for part in (u[:, :w], u[:, w:]):
                hi, lo = _split_bf16(part)
                halves.append(_dot(jnp.concatenate([hi, lo], axis=1), scan))
            zs.append(z)
            us.append(u)
            scanned.append(halves)
        weights = []
        for hh in heads:
            (left, right), run = scanned[hh], run_scr[hh]
            right_of_left = right[:, w:] + run
            s_left = us[hh][:, :w] + left[:, :w] + right_of_left
            s_right = us[hh][:, w:] + right[:, :w] + run
            e = jnp.exp(zs[hh] - jnp.concatenate([s_left, s_right], axis=1))
            if diagonal:
                e = jnp.where(causal, e, 0.0)
            weights.append(e.astype(BF16))
            run_scr[hh] = left[:, w:] + right_of_left
        for hh in heads:
            acc_scr[hh] += _dot(weights[hh], v_ref[pl.ds(start, t), cols[hh]].astype(BF16))

    block(qi, True)

    def body(i, carry):
        block(qi - 1 - i, False)
        return carry

    lax.fori_loop(0, qi, body, 0)
    for hh in heads:
        o_ref[:, cols[hh]] = acc_scr[hh].astype(o_ref.dtype)


def sb_prompt(proj, sk, q_gain, bias, layer, batch, seq):
    t = SB_TILE
    nq = seq // t
    hs = SB_HEADS_PER_STEP
    width = hs * HEAD_DIM
    groups = HEADS // hs
    return pl.pallas_call(
        _sb_prompt_kernel,
        grid_spec=pltpu.PrefetchScalarGridSpec(
            num_scalar_prefetch=1,
            grid=(batch, groups, nq),
            in_specs=[pl.BlockSpec((t, width), lambda b, g, qi, s: (b * nq + qi, COL_SQ * groups + g)),
                      pl.BlockSpec((None, seq, width), lambda b, g, qi, s: (layer, b, g)),
                      pl.BlockSpec((seq, width), lambda b, g, qi, s: (b, COL_SV * groups + g)),
                      pl.BlockSpec((None, 1, HEAD_DIM), lambda b, g, qi, s: (layer, 0, 0))],
            out_specs=pl.BlockSpec((t, width), lambda b, g, qi, s: (b * nq + qi, g)),
            scratch_shapes=[pltpu.VMEM((hs, t, HEAD_DIM), F32), pltpu.VMEM((hs, t, HEAD_DIM), F32)]),
        out_shape=jax.ShapeDtypeStruct((batch * seq, MIX_W), BF16),
        compiler_params=_params("parallel", "parallel", "parallel"),
        name="sb_prompt",
    )(bias, proj, sk, proj, q_gain.reshape(DEPTH, 1, HEAD_DIM))


def _decode_mix_kernel(lg_ref, p_ref, s_ref, cprev_ref, cos_ref, sin_ref, gn_ref, cw_ref, qg_ref, kg_ref,
                       ya_ref, yb_ref, sq_ref, sk_ref, s_out_ref, cn_ref):
    cos2 = cos_ref[...]
    sin2 = sin_ref[...]
    row = lax.broadcasted_iota(jnp.int32, (HEAD_DIM, HEAD_DIM), 0)
    col = lax.broadcasted_iota(jnp.int32, (HEAD_DIM, HEAD_DIM), 1)

    def stream(c, h):
        return p_ref[:, c * MIX_W + h * HEAD_DIM:c * MIX_W + (h + 1) * HEAD_DIM]

    def bf_round(x):
        return x.astype(BF16).astype(F32)

    for h in range(HEADS):
        cols = slice(h * HEAD_DIM, (h + 1) * HEAD_DIM)
        lg = lg_ref[h]
        q = bf_round(_rope(stream(COL_RQ, h), cos2, sin2))
        k = bf_round(_rope(stream(COL_RK, h), cos2, sin2) * (HEAD_DIM ** -0.5))
        v = bf_round(stream(COL_RV, h))
        s = s_ref[h]
        score = bf_round(jnp.sum(q * k, axis=-1, keepdims=True))
        inner = score * v
        q_rows = jnp.broadcast_to(q, (16, HEAD_DIM)).astype(BF16)
        cross = _dot(q_rows, s.astype(BF16))[0:1, :] * jnp.exp(lg * jnp.ones((1, 1), F32))
        k_col = jnp.sum(jnp.where(row == col, jnp.broadcast_to(k, (HEAD_DIM, HEAD_DIM)), 0.0),
                        axis=1, keepdims=True)
        s_out_ref[h] = jnp.exp(lg * jnp.ones((1, 1), F32)) * s + k_col * v
        ya_ref[:, cols] = _rms(inner + cross, gn_ref[h:h + 1, :]).astype(ya_ref.dtype)
        sq_ref[:, cols] = _rms(stream(COL_SQ, h), qg_ref[...])
        sk_ref[:, cols] = _rms(stream(COL_SK, h), kg_ref[...])

    ch = p_ref[:, COL_CH * MIX_W:(COL_CH + 1) * MIX_W]
    cb = p_ref[:, COL_CB * MIX_W:(COL_CB + 1) * MIX_W]
    cc = p_ref[:, COL_CC * MIX_W:(COL_CC + 1) * MIX_W]
    u = cc * ch
    cv = cprev_ref[0:1, :] * cw_ref[0:1, :] + cprev_ref[1:2, :] * cw_ref[1:2, :] + u * cw_ref[2:3, :]
    yb_ref[...] = (cb * cv).astype(yb_ref.dtype)
    cn_ref[0:1, :] = cprev_ref[1:2, :]
    cn_ref[1:2, :] = u


def decode_mix(proj_s, state_ret, state_conv, cos2, sin2, lg, gn, conv_w, q_gain, k_gain, layer, nb):
    def row_spec(width):
        return pl.BlockSpec((None, 1, width), lambda b, s: (b, 0, 0))

    def w_spec(*shape):
        return pl.BlockSpec((None,) + shape, lambda b, s: (layer,) + (0,) * len(shape))

    tab_spec = pl.BlockSpec((1, HEAD_DIM), lambda b, s: (0, 0))
    return pl.pallas_call(
        _decode_mix_kernel,
        grid_spec=pltpu.PrefetchScalarGridSpec(
            num_scalar_prefetch=1,
            grid=(nb,),
            in_specs=[row_spec(N_IN),
                      pl.BlockSpec((None, None, HEADS, HEAD_DIM, HEAD_DIM), lambda b, s: (layer, b, 0, 0, 0)),
                      pl.BlockSpec((None, None, 2, MIX_W), lambda b, s: (layer, b, 0, 0)),
                      tab_spec, tab_spec, w_spec(HEADS, HEAD_DIM), w_spec(3, MIX_W),
                      w_spec(1, HEAD_DIM), w_spec(1, HEAD_DIM)],
            out_specs=[row_spec(MIX_W), row_spec(MIX_W), row_spec(MIX_W), row_spec(MIX_W),
                       pl.BlockSpec((None, HEADS, HEAD_DIM, HEAD_DIM), lambda b, s: (b, 0, 0, 0)),
                       pl.BlockSpec((None, 2, MIX_W), lambda b, s: (b, 0, 0))]),
        out_shape=[jax.ShapeDtypeStruct((nb, 1, MIX_W), BF16),
                   jax.ShapeDtypeStruct((nb, 1, MIX_W), BF16),
                   jax.ShapeDtypeStruct((nb, 1, MIX_W), F32),
                   jax.ShapeDtypeStruct((nb, 1, MIX_W), F32),
                   jax.ShapeDtypeStruct((nb, HEADS, HEAD_DIM, HEAD_DIM), F32),
                   jax.ShapeDtypeStruct((nb, 2, MIX_W), F32)],
        compiler_params=_params("parallel"),
        name="decode_mix",
    )(lg, proj_s, state_ret, state_conv, cos2, sin2, gn, conv_w,
      q_gain.reshape(DEPTH, 1, HEAD_DIM), k_gain.reshape(DEPTH, 1, HEAD_DIM))


PAGES_PER_STEP = 8
WINDOW = PAGES_PER_STEP * PAGE_SIZE
LANES = 128
SUB_ROWS = 16


def _sb_decode_kernel(pt_ref, q_ref, bias_ref, *refs, n_steps):
    g_n = PAGES_PER_STEP
    k_refs, v_refs = refs[:g_n], refs[g_n:2 * g_n]
    o_ref, qb_scr, scan_scr, acc_scr, run_scr = refs[2 * g_n:]
    s = pl.program_id(1)
    head_row = lax.broadcasted_iota(jnp.int32, (SUB_ROWS, WINDOW), 0)

    def head_window(page_refs, h):
        return jnp.concatenate(
            [r[pl.ds(h, PAGE_SIZE, stride=HEADS), :].astype(BF16) for r in page_refs], axis=0)

    @pl.when(s == 0)
    def _():
        acc_scr[...] = jnp.zeros_like(acc_scr)
        run_scr[...] = jnp.zeros_like(run_scr)
        q = q_ref[...]
        for h in range(HEADS):
            qb_scr[:, h * HEAD_DIM:(h + 1) * HEAD_DIM] = jnp.where(head_row[:, :HEAD_DIM] == h, q, 0.0).astype(BF16)
        r2 = lax.broadcasted_iota(jnp.int32, (WINDOW, WINDOW + LANES), 0)
        c2 = lax.broadcasted_iota(jnp.int32, (WINDOW, WINDOW + LANES), 1)
        scan_scr[...] = jnp.where((r2 > c2) | (c2 >= WINDOW), 1.0, 0.0).astype(BF16)

    k_all = jnp.concatenate([head_window(k_refs, h) for h in range(HEADS)], axis=1)
    z = _dot_nt(qb_scr[...], k_all) * (HEAD_DIM ** -0.5) + bias_ref[...]
    sp = _softplus_neg_abs(z)
    log_b = jnp.minimum(z, 0.0) - sp
    log_1mb = -jnp.maximum(z, 0.0) - sp
    hi, lo = _split_bf16(log_1mb)
    r = _dot(jnp.concatenate([hi, lo], axis=0), scan_scr[...])
    r = r[:SUB_ROWS, :] + r[SUB_ROWS:, :]
    run = run_scr[...]
    a = jnp.exp(log_b + r[:, :WINDOW] + jnp.concatenate([run] * (WINDOW // LANES), axis=1))
    run_scr[...] = run + r[:, WINDOW:]

    a_all = jnp.concatenate([jnp.where(head_row == h, a, 0.0).astype(BF16) for h in range(HEADS)], axis=1)
    v_all = jnp.concatenate([head_window(v_refs, h) for h in range(HEADS)], axis=0)
    acc = acc_scr[...] + _dot(a_all, v_all)
    acc_scr[...] = acc

    @pl.when(s == n_steps - 1)
    def _():
        o_ref[...] = acc[0:HEADS, :]


def sb_decode(q_rows, bias_col, cache_k, cache_v, page_table, layer):
    nb, n_pages = page_table.shape
    assert n_pages % PAGES_PER_STEP == 0, (n_pages, PAGES_PER_STEP)
    n_steps = n_pages // PAGES_PER_STEP

    def page_spec(g):
        return pl.BlockSpec(
            (None, None, PAGE_SIZE * HEADS, HEAD_DIM),
            lambda b, s, pt: (layer, pt[b, n_pages - (s + 1) * PAGES_PER_STEP + g], 0, 0))

    pages = [page_spec(g) for g in range(PAGES_PER_STEP)]
    return pl.pallas_call(
        functools.partial(_sb_decode_kernel, n_steps=n_steps),
        grid_spec=pltpu.PrefetchScalarGridSpec(
            num_scalar_prefetch=1,
            grid=(nb, n_steps),
            in_specs=[pl.BlockSpec((None, SUB_ROWS, HEAD_DIM), lambda b, s, pt: (b, 0, 0)),
                      pl.BlockSpec((None, SUB_ROWS, 1), lambda b, s, pt: (layer, 0, 0))] + pages + pages,
            out_specs=pl.BlockSpec((None, HEADS, HEAD_DIM), lambda b, s, pt: (b, 0, 0)),
            scratch_shapes=[pltpu.VMEM((SUB_ROWS, HEADS * HEAD_DIM), BF16),
                            pltpu.VMEM((WINDOW, WINDOW + LANES), BF16),
                            pltpu.VMEM((SUB_ROWS, HEAD_DIM), F32),
                            pltpu.VMEM((SUB_ROWS, LANES), F32)]),
        out_shape=jax.ShapeDtypeStruct((nb, HEADS, HEAD_DIM), F32),
        compiler_params=_params("parallel", "arbitrary"),
        name="sb_decode",
    )(page_table, q_rows, bias_col, *([cache_k] * PAGES_PER_STEP), *([cache_v] * PAGES_PER_STEP))


def _rope_tables(pos):
    half = HEAD_DIM // 2
    inv = ROPE_THETA ** (-jnp.arange(half, dtype=F32) / half)
    ang = pos.astype(F32)[:, None] * inv[None, :]
    cos, sin = jnp.cos(ang), jnp.sin(ang)
    return jnp.concatenate([cos, cos], axis=-1), jnp.concatenate([-sin, sin], axis=-1)


def kernel(x_prompt, x_sample, cache_sb_k, cache_sb_v, state_ret, state_conv, page_table, p_prompt, p_sample,
           ffn1_norm, ffn1_w_gu, ffn1_w_down, mix_norm, w_in, ret_gn, conv_w, sb_q_norm, sb_k_norm, sb_bias,
           w_branch_ret, w_branch_conv, w_branch_sb, w_out, ffn2_norm, ffn2_w_gu, ffn2_w_down,
           ple_norm, w_ple_gate, w_ple_up):
    batch, seq, _ = x_prompt.shape
    nb = x_sample.shape[0]
    m_p = batch * seq
    m_s = 16

    lg =jnp.log1p(-jnp.exp2(-5.0 - jnp.arange(HEADS, dtype=F32)))
    cos_p, sin_p = _rope_tables(jnp.arange(seq, dtype=jnp.int32))
    past_len = page_table.shape[1] * PAGE_SIZE
    cos_s, sin_s = _rope_tables(jnp.full((1,), past_len, jnp.int32))
    tp = lambda t: min(t, m_p)
    page_rows = cache_sb_k.shape[:2] + (PAGE_SIZE * HEADS, HEAD_DIM)
    cache_sb_k, cache_sb_v = cache_sb_k.reshape(page_rows), cache_sb_v.reshape(page_rows)
    row_pad = ((0, 0), (0, SUB_ROWS - HEADS), (0, 0))
    bias_cols = jnp.pad(sb_bias.reshape(DEPTH, HEADS, 1), row_pad)

    xp = x_prompt.reshape(m_p, D_MODEL)
    xs = jnp.pad(x_sample.reshape(nb, D_MODEL), ((0, m_s - nb), (0, 0)))
    pp = p_prompt.reshape(DEPTH, m_p, P_DIM)
    ps = jnp.pad(p_sample.reshape(DEPTH, nb, P_DIM), ((0, 0), (0, m_s - nb), (0, 0)))

    def ffn(stream, w_gu, w_down, next_gain, i):
        xp, xs, gp, gs, qp, qs = stream
        ap, a_s = dense("swiglu_mm", _swiglu_body, [(gp, None)], [(gs, None)],
                        [(w_gu, 0), (w_gu, D_FF)], i, D_FF, BF16, norm_in=(qp, qs))
        return dense("ffn_down", functools.partial(_residual_body, scale=0.5),
                     [(ap, None), (xp, 0)], [(a_s, None), (xs, 0)], [(w_down, 0)], i, D_MODEL, F32,
                     norm_out=(next_gain, i))

    stream = (xp, xs) + tuple(prenorm(xp, xs, ffn1_norm, 0, tp(256)))
    outs = [[] for _ in range(6)]
    kv_p = None
    for i in range(DEPTH):
        stream = ffn(stream, ffn1_w_gu, ffn1_w_down, mix_norm, i)
        xp, xs, gp, gs, qp, qs = stream
        proj, proj_s = dense("in_proj", _proj_body, [(gp, None)], [(gs, None)], [(w_in, 0)], i, N_IN, F32,
                             norm_in=(qp, qs))

        ya, s_p = retention_prompt(proj, cos_p, sin_p, lg, ret_gn, i, batch, seq)
        yb, c_p = conv_prompt(proj, conv_w, i, batch, seq)
        kv_p = new_kv(proj, sb_k_norm, i, kv_p, tp(512))
        yc = sb_prompt(proj, kv_p[0], sb_q_norm, sb_bias[i], i, batch, seq)

        ya_s, yb_s, sq_s, sk_s, s_s, c_s = decode_mix(
            proj_s[:nb].reshape(nb, 1, N_IN), state_ret, state_conv, cos_s, sin_s, lg, ret_gn, conv_w,
            sb_q_norm, sb_k_norm, i, nb)
        q_rows = jnp.pad(sq_s.reshape(nb, HEADS, HEAD_DIM), row_pad)
        yc_s = sb_decode(q_rows, bias_cols, cache_sb_k, cache_sb_v, page_table, i)
        yc_s = yc_s.reshape(nb, MIX_W).astype(BF16)
        pad_rows = lambda y: jnp.pad(y.reshape(nb, MIX_W), ((0, m_s - nb), (0, 0)))
        sv_s = proj_s[:nb, COL_SV * MIX_W:(COL_SV + 1) * MIX_W]

        def branch_rows(ya, yb, yc, proj):
            return [(ya, None), (yb, None), (yc, None),
                    (proj, COL_GA * MIX_W), (proj, COL_GB * MIX_W), (proj, COL_GC * MIX_W)]

        mp, ms = dense("merge_branches", _merge_body, branch_rows(ya, yb, yc, proj),
                       branch_rows(pad_rows(ya_s), pad_rows(yb_s), pad_rows(yc_s), proj_s),
                       [(w_branch_ret, 0), (w_branch_conv, 0), (w_branch_sb, 0)], i, D_MODEL, BF16)
        stream = dense("out_proj", functools.partial(_residual_body, scale=1.0),
                       [(mp, None), (xp, 0)], [(ms, None), (xs, 0)], [(w_out, 0)], i, D_MODEL, F32,
                       norm_out=(ffn2_norm, i))
        xp, xs, gp, gs, qp, qs = ffn(stream, ffn2_w_gu, ffn2_w_down, ple_norm, i)
        stream = dense("ple_update", _ple_body, [(gp, None), (pp[i], None), (xp, 0)],
                       [(gs, None), (ps[i], None), (xs, 0)], [(w_ple_gate, 0), (w_ple_up, 0)],
                       i, D_MODEL, F32, norm_in=(qp, qs),
                       norm_out=(ffn1_norm, i + 1) if i + 1 < DEPTH else None)
        xp, xs = stream[:2]

        for lst, val in zip(outs, (sk_s.reshape(nb, 1, HEADS, HEAD_DIM),
                                   sv_s.reshape(nb, 1, HEADS, HEAD_DIM),
                                   s_p, s_s, c_p, c_s)):
            lst.append(val)

    new_k_p, new_v_p = (a.reshape(DEPTH, batch, seq, HEADS, HEAD_DIM) for a in kv_p)
    return ((xp.reshape(batch, seq, D_MODEL), xs[:nb].reshape(nb, 1, D_MODEL), new_k_p, new_v_p)
            + tuple(jnp.stack(o) for o in outs))
```

```python
import functools

import jax
import jax.numpy as jnp
from jax import lax
from jax.experimental import pallas as pl
from jax.experimental.pallas import tpu as pltpu

F32 = jnp.float32
BF16 = jnp.bfloat16

D_MODEL = 2048
DEPTH = 2
PAGE_SIZE = 128
HEADS = 8
HEAD_DIM = 128
MIX_W = HEADS * HEAD_DIM
CHUNK = 128
D_FF = 5632
P_DIM = 256
ROPE_THETA = 10000.0
EPS = 1e-6
N_IN = 9 * MIX_W + 3 * D_MODEL
COL_RQ, COL_RK, COL_RV, COL_CH, COL_CB, COL_CC, COL_SQ, COL_SK, COL_SV = range(9)
COL_GA, COL_GB, COL_GC = 9, 11, 13

VMEM_LIMIT_BYTES = 56 * 1024 * 1024


def _params(*sem):
    return pltpu.CompilerParams(dimension_semantics=sem, vmem_limit_bytes=VMEM_LIMIT_BYTES)


def _dot(a, b):
    return jnp.dot(a, b, preferred_element_type=F32)


def _dot_nt(a, b):
    return lax.dot_general(a, b, (((1,), (1,)), ((), ())), preferred_element_type=F32)


def _dot_tn(a, b):
    return lax.dot_general(a, b, (((0,), (0,)), ((), ())), preferred_element_type=F32)


def _rms(x, g):
    return x * lax.rsqrt(jnp.mean(x * x, axis=-1, keepdims=True) + EPS) * g


def _sigmoid(x):
    return jax.nn.sigmoid(x)


def _row_scale(sumsq_parts):
    return lax.rsqrt(jnp.sum(sumsq_parts, axis=0) * (1.0 / D_MODEL) + EPS)


def _scaled(acc, r):
    return acc * jnp.concatenate([r] * (acc.shape[1] // LANES), axis=1)


def _gain_and_sumsq(x, gain, gained_ref, sumsq_ref):
    gained_ref[...] = (x * gain).astype(gained_ref.dtype)
    sumsq_ref[...] = jnp.broadcast_to(jnp.sum(x * x, axis=1, keepdims=True), sumsq_ref.shape)


def _prenorm_kernel(xp_ref, xs_ref, g_ref, gp_ref, gs_ref, qp_ref, qs_ref):
    _gain_and_sumsq(xp_ref[...], g_ref[...], gp_ref, qp_ref)

    @pl.when(pl.program_id(0) == 0)
    def _():
        _gain_and_sumsq(xs_ref[...], g_ref[...], gs_ref, qs_ref)


def prenorm(xp, xs, g, layer, tm):
    m, d = xp.shape
    ms = xs.shape[0]
    return pl.pallas_call(
        _prenorm_kernel,
        grid=(m // tm,),
        in_specs=[pl.BlockSpec((tm, d), lambda i: (i, 0)),
                  pl.BlockSpec((ms, d), lambda i: (0, 0)),
                  pl.BlockSpec((None, 1, d), lambda i: (layer, 0, 0))],
        out_specs=[pl.BlockSpec((tm, d), lambda i: (i, 0)),
                   pl.BlockSpec((ms, d), lambda i: (0, 0)),
                   pl.BlockSpec((None, tm, LANES), lambda i: (0, i, 0)),
                   pl.BlockSpec((None, ms, LANES), lambda i: (0, 0, 0))],
        out_shape=[jax.ShapeDtypeStruct((m, d), BF16), jax.ShapeDtypeStruct((ms, d), BF16),
                   jax.ShapeDtypeStruct((1, m, LANES), F32), jax.ShapeDtypeStruct((1, ms, LANES), F32)],
        compiler_params=_params("arbitrary"),
        name="prenorm",
    )(xp, xs, g.reshape(DEPTH, 1, d))


def _dense_kernel(*refs, body, n_rows, n_w, norm_in, norm_out):
    refs = iter(refs)
    take = lambda n: [next(refs) for _ in range(n)]
    rows_p, rows_s = take(n_rows), take(n_rows)
    sumsq_in = take(2) if norm_in else None
    first_refs, next_refs = take(n_w), take(n_w)
    gain_ref = take(1)[0] if norm_out else None
    op_ref, os_ref = take(2)
    norm_refs = take(4) if norm_out else None
    w_scr = take(n_w)
    scale_scr = take(2) if norm_in else None
    j, i = pl.program_id(0), pl.program_id(1)
    cur = j % 2

    @pl.when((j == 0) & (i == 0))
    def _():
        for first, scr in zip(first_refs, w_scr):
            scr[0] = first[...].astype(BF16)
        if norm_in:
            scale_scr[1][...] = _row_scale(sumsq_in[1][...])

    for nxt, scr in zip(next_refs, w_scr):
        rows = nxt.shape[0]
        scr[1 - cur, pl.ds(pl.multiple_of(i * rows, rows), rows), :] = nxt[...].astype(BF16)

    if norm_in:
        @pl.when(j == 0)
        def _():
            scale_scr[0][i] = _row_scale(sumsq_in[0][...])

    def emit(rows, scale, o_ref, gained_ref, sumsq_ref):
        out = body([r[...] for r in rows], [scr[cur] for scr in w_scr], scale)
        o_ref[...] = out.astype(o_ref.dtype)
        if norm_out:
            _gain_and_sumsq(out, gain_ref[...], gained_ref, sumsq_ref)

    @pl.when(i == 0)
    def _():
        emit(rows_s, scale_scr[1][...] if norm_in else None, os_ref,
             norm_refs[1] if norm_out else None, norm_refs[3] if norm_out else None)

    emit(rows_p, scale_scr[0][i] if norm_in else None, op_ref,
         norm_refs[0] if norm_out else None, norm_refs[2] if norm_out else None)


DENSE_TILES = {
    "swiglu_mm": (1024, 512), "ffn_down": (512, 512), "in_proj": (1024, 1024),
    "merge_branches": (512, 1024), "out_proj": (512, 1024), "ple_update": (512, 1024),
}


def dense(name, body, rows_p, rows_s, weights, layer, n_out, out_dtype, norm_in=None, norm_out=None):
    tm, tn = DENSE_TILES[name]
    rows_p = [(a, c if c is None else c // tn) for a, c in rows_p]
    rows_s = [(a, c if c is None else c // tn) for a, c in rows_s]
    weights = [(a, c // tn) for a, c in weights]
    m_p, m_s = rows_p[0][0].shape[0], rows_s[0][0].shape[0]
    tm = min(tm, m_p)

    def row_specs(rows, t, tiled):
        specs = []
        for arr, col in rows:
            if col is None:
                specs.append(pl.BlockSpec((t, arr.shape[1]), (lambda j, i: (i, 0)) if tiled else (lambda j, i: (0, 0))))
            elif tiled:
                specs.append(pl.BlockSpec((t, tn), lambda j, i, col=col: (i, col + j)))
            else:
                specs.append(pl.BlockSpec((t, tn), lambda j, i, col=col: (0, col + j)))
        return specs

    n_j, n_i = n_out // tn, m_p // tm
    first_specs = [pl.BlockSpec((None, w.shape[1], tn), lambda j, i, col=col: (layer, 0, col),
                                pipeline_mode=pl.Buffered(1)) for w, col in weights]
    for w, _ in weights:
        assert w.shape[1] % (n_i * SUB_ROWS) == 0, (name, w.shape, n_i)
    next_specs = [pl.BlockSpec((None, w.shape[1] // n_i, tn),
                               lambda j, i, col=col: (layer, i, col + jnp.minimum(j + 1, n_j - 1)))
                  for w, col in weights]
    tile_p = pl.BlockSpec((tm, tn), lambda j, i: (i, j))
    tile_s = pl.BlockSpec((m_s, tn), lambda j, i: (0, j))
    in_specs = row_specs(rows_p, tm, True) + row_specs(rows_s, m_s, False)
    operands = [a for a, _ in rows_p] + [a for a, _ in rows_s]
    out_specs, out_shape = [tile_p, tile_s], [jax.ShapeDtypeStruct((m_p, n_out), out_dtype),
                                              jax.ShapeDtypeStruct((m_s, n_out), out_dtype)]
    scratch = [pltpu.VMEM((2, w.shape[1], tn), BF16) for w, _ in weights]
    if norm_in is not None:
        parts = norm_in[0].shape[0]
        in_specs += [pl.BlockSpec((parts, tm, LANES), lambda j, i: (0, jnp.where(j == 0, i, n_i - 1), 0)),
                     pl.BlockSpec((parts, m_s, LANES), lambda j, i: (0, 0, 0))]
        operands += list(norm_in)
        scratch += [pltpu.VMEM((n_i, tm, LANES), F32), pltpu.VMEM((m_s, LANES), F32)]
    in_specs += first_specs + next_specs
    operands += [w for w, _ in weights] * 2
    if norm_out is not None:
        gain, gain_layer = norm_out
        in_specs.append(pl.BlockSpec((None, 1, tn), lambda j, i: (gain_layer, 0, j)))
        operands.append(gain.reshape(DEPTH, 1, n_out))
        out_specs += [tile_p, tile_s, pl.BlockSpec((None, tm, LANES), lambda j, i: (j, i, 0)),
                      pl.BlockSpec((None, m_s, LANES), lambda j, i: (j, 0, 0))]
        out_shape += [jax.ShapeDtypeStruct((m_p, n_out), BF16), jax.ShapeDtypeStruct((m_s, n_out), BF16),
                      jax.ShapeDtypeStruct((n_j, m_p, LANES), F32), jax.ShapeDtypeStruct((n_j, m_s, LANES), F32)]
    return pl.pallas_call(
        functools.partial(_dense_kernel, body=body, n_rows=len(rows_p), n_w=len(weights),
                          norm_in=norm_in is not None, norm_out=norm_out is not None),
        grid=(n_j, n_i),
        in_specs=in_specs,
        out_specs=out_specs,
        out_shape=out_shape,
        scratch_shapes=scratch,
        compiler_params=_params("arbitrary", "arbitrary"),
        name=name,
    )(*operands)


def _swiglu_body(rows, ws, r):
    (h,), (wg, wu) = rows, ws
    g = _scaled(_dot(h, wg), r)
    return g * _sigmoid(g) * _scaled(_dot(h, wu), r)


def _residual_body(rows, ws, r, *, scale):
    (a, x), (w,) = rows, ws
    return x + scale * _dot(a, w)


def _proj_body(rows, ws, r):
    return _scaled(_dot(rows[0], ws[0]), r)


def _merge_body(rows, ws, r):
    (ya, yb, yc, ga, gb, gc), (wr, wc, wsb) = rows, ws
    return _sigmoid(ga) * _dot(ya, wr) + _sigmoid(gb) * _dot(yb, wc) + _sigmoid(gc) * _dot(yc, wsb)


def _ple_body(rows, ws, r):
    (h, p, x), (wg, wu) = rows, ws
    return x + _sigmoid(_scaled(_dot(h, wg), r)) * _dot(p.astype(BF16), wu)


def _rope(x, cos2, sin2):
    return x * cos2 + pltpu.roll(x, HEAD_DIM // 2, 1) * sin2


def _softplus_neg_abs(z):
    neg_abs = lax.bitcast_convert_type(
        lax.bitcast_convert_type(z, jnp.uint32) | jnp.uint32(0x80000000), F32)
    return jnp.log(1.0 + jnp.exp(neg_abs))


def _split_bf16(x):
    hi = x.astype(BF16)
    return hi, (x - hi.astype(F32)).astype(BF16)


def _retention_kernel(lg_ref, q_ref, k_ref, v_ref, cos_ref, sin_ref, gn_ref, ya_ref, s_out_ref, s_scr,
                      *, n_chunks):
    c = pl.program_id(1)

    @pl.when(c == 0)
    def _():
        s_scr[...] = jnp.zeros_like(s_scr)

    cos2 = cos_ref[...]
    sin2 = sin_ref[...]
    ii = lax.broadcasted_iota(jnp.int32, (CHUNK, CHUNK), 0)
    jj = lax.broadcasted_iota(jnp.int32, (CHUNK, CHUNK), 1)
    diff = (ii - jj).astype(F32)
    pos = lax.broadcasted_iota(jnp.int32, (CHUNK, 1), 0).astype(F32)
    heads = range(HEADS)
    cols = [slice(h * HEAD_DIM, (h + 1) * HEAD_DIM) for h in heads]
    lg = [lg_ref[h] for h in heads]
    qb, kb, kd, vb = [], [], [], []
    for h in heads:
        k = _rope(k_ref[:, cols[h]], cos2, sin2) * (HEAD_DIM ** -0.5)
        qb.append(_rope(q_ref[:, cols[h]], cos2, sin2).astype(BF16))
        kb.append(k.astype(BF16))
        kd.append((k * jnp.exp(lg[h] * (CHUNK - 1.0 - pos))).astype(BF16))
        vb.append(v_ref[:, cols[h]].astype(BF16))
    state = [s_scr[h] for h in heads]
    scores = [_dot_nt(qb[h], kb[h]) for h in heads]
    cross = [_dot(qb[h], state[h].astype(BF16)) for h in heads]
    update = [_dot_tn(kd[h], vb[h]) for h in heads]
    inner = []
    for h in heads:
        dmask = jnp.where(diff >= 0, jnp.exp(lg[h] * jnp.maximum(diff, 0.0)), 0.0)
        inner.append(_dot((scores[h] * dmask).astype(BF16), vb[h]))
    for h in heads:
        o = inner[h] + cross[h] * jnp.exp(lg[h] * (pos + 1.0))
        ya_ref[:, cols[h]] = _rms(o, gn_ref[h:h + 1, :]).astype(ya_ref.dtype)
        s_scr[h] = jnp.exp(lg[h] * CHUNK) * state[h] + update[h]

    @pl.when(c == n_chunks - 1)
    def _():
        s_out_ref[...] = s_scr[...]


def retention_prompt(proj, cos2, sin2, lg, gn, layer, batch, seq):
    n_chunks = seq // CHUNK
    m = batch * seq

    def col_spec(col):
        return pl.BlockSpec((CHUNK, MIX_W), lambda b, c, lg_ref: (b * n_chunks + c, col))

    tab_spec = pl.BlockSpec((CHUNK, HEAD_DIM), lambda b, c, lg_ref: (c, 0))
    return pl.pallas_call(
        functools.partial(_retention_kernel, n_chunks=n_chunks),
        grid_spec=pltpu.PrefetchScalarGridSpec(
            num_scalar_prefetch=1,
            grid=(batch, n_chunks),
            in_specs=[col_spec(COL_RQ), col_spec(COL_RK), col_spec(COL_RV), tab_spec, tab_spec,
                      pl.BlockSpec((None, HEADS, HEAD_DIM), lambda b, c, lg_ref: (layer, 0, 0))],
            out_specs=[pl.BlockSpec((CHUNK, MIX_W), lambda b, c, lg_ref: (b * n_chunks + c, 0)),
                       pl.BlockSpec((None, HEADS, HEAD_DIM, HEAD_DIM), lambda b, c, lg_ref: (b, 0, 0, 0))],
            scratch_shapes=[pltpu.VMEM((HEADS, HEAD_DIM, HEAD_DIM), F32)]),
        out_shape=[jax.ShapeDtypeStruct((m, MIX_W), BF16),
                   jax.ShapeDtypeStruct((batch, HEADS, HEAD_DIM, HEAD_DIM), F32)],
        compiler_params=_params("parallel", "arbitrary"),
        name="retention_prompt",
    )(lg, proj, proj, proj, cos2, sin2, gn)


def _conv_kernel(ch_ref, cb_ref, cc_ref, w_ref, yb_ref, cn_ref):
    u = cc_ref[...] * ch_ref[...]
    seq = u.shape[0]
    t = lax.broadcasted_iota(jnp.int32, u.shape, 0)
    u1 = jnp.where(t >= 1, pltpu.roll(u, 1, 0), 0.0)
    u2 = jnp.where(t >= 2, pltpu.roll(u, 2, 0), 0.0)
    cv = u2 * w_ref[0:1, :] + u1 * w_ref[1:2, :] + u * w_ref[2:3, :]
    yb_ref[...] = (cb_ref[...] * cv).astype(yb_ref.dtype)
    cn_ref[...] = cc_ref[seq - 2:seq, :] * ch_ref[seq - 2:seq, :]


def conv_prompt(proj, conv_w, layer, batch, seq):
    tw = 256
    nw = MIX_W // tw

    def col_spec(col):
        return pl.BlockSpec((seq, tw), lambda b, j: (b, col * nw + j))

    return pl.pallas_call(
        _conv_kernel,
        grid=(batch, nw),
        in_specs=[col_spec(COL_CH), col_spec(COL_CB), col_spec(COL_CC),
                  pl.BlockSpec((None, 3, tw), lambda b, j: (layer, 0, j))],
        out_specs=[pl.BlockSpec((seq, tw), lambda b, j: (b, j)),
                   pl.BlockSpec((None, 2, tw), lambda b, j: (b, 0, j))],
        out_shape=[jax.ShapeDtypeStruct((batch * seq, MIX_W), BF16),
                   jax.ShapeDtypeStruct((batch, 2, MIX_W), F32)],
        compiler_params=_params("parallel", "parallel"),
        name="conv_prompt",
    )(proj, proj, proj, conv_w)


def _new_kv_kernel(k_ref, v_ref, g_ref, *refs, layer, first):
    ok_ref, ov_ref = refs[-2:]

    def write_layer():
        g = g_ref[...]
        for h in range(HEADS):
            cols = slice(h * HEAD_DIM, (h + 1) * HEAD_DIM)
            ok_ref[:, cols] = _rms(k_ref[:, cols], g)
        ov_ref[...] = v_ref[...]

    if first:
        pl.when(pl.program_id(0) == layer)(write_layer)

        @pl.when(pl.program_id(0) != layer)
        def _():
            ok_ref[...] = jnp.zeros_like(ok_ref)
            ov_ref[...] = jnp.zeros_like(ov_ref)
    else:
        write_layer()


def new_kv(proj, g, layer, prev, tm):
    m = proj.shape[0]
    n_i = m // tm
    first = prev is None
    if first:
        grid = (DEPTH, n_i)
        row = lambda l, i: jnp.where(l == layer, i, 0)
        in_specs = [pl.BlockSpec((tm, MIX_W), lambda l, i: (row(l, i), COL_SK)),
                    pl.BlockSpec((tm, MIX_W), lambda l, i: (row(l, i), COL_SV)),
                    pl.BlockSpec((None, 1, HEAD_DIM), lambda l, i: (layer, 0, 0))]
        out_spec = pl.BlockSpec((None, tm, MIX_W), lambda l, i: (l, i, 0))
        operands, aliases, sem = [proj, proj, g.reshape(DEPTH, 1, HEAD_DIM)], {}, ("arbitrary", "arbitrary")
    else:
        grid = (n_i,)
        in_specs = [pl.BlockSpec((tm, MIX_W), lambda i: (i, COL_SK)),
                    pl.BlockSpec((tm, MIX_W), lambda i: (i, COL_SV)),
                    pl.BlockSpec((None, 1, HEAD_DIM), lambda i: (layer, 0, 0)),
                    pl.BlockSpec(memory_space=pl.ANY), pl.BlockSpec(memory_space=pl.ANY)]
        out_spec = pl.BlockSpec((None, tm, MIX_W), lambda i: (layer, i, 0))
        operands = [proj, proj, g.reshape(DEPTH, 1, HEAD_DIM), prev[0], prev[1]]
        aliases, sem = {3: 0, 4: 1}, ("arbitrary",)
    return pl.pallas_call(
        functools.partial(_new_kv_kernel, layer=layer, first=first),
        grid=grid,
        in_specs=in_specs,
        out_specs=[out_spec, out_spec],
        out_shape=[jax.ShapeDtypeStruct((DEPTH, m, MIX_W), F32)] * 2,
        input_output_aliases=aliases,
        compiler_params=_params(*sem),
        name="new_kv",
    )(*operands)


SB_TILE = 256


SB_HEADS_PER_STEP = 8


def _sb_prompt_kernel(bias_ref, q_ref, k_ref, v_ref, qg_ref, o_ref, acc_scr, run_scr):
    t = SB_TILE
    w = HEAD_DIM
    hg = pl.program_id(1)
    qi = pl.program_id(2)
    row = lax.broadcasted_iota(jnp.int32, (t, t), 0)
    col = lax.broadcasted_iota(jnp.int32, (t, t), 1)
    causal = col < row
    key = jnp.bitwise_and(lax.broadcasted_iota(jnp.int32, (2 * w, 2 * w), 0), w - 1)
    c2 = lax.broadcasted_iota(jnp.int32, (2 * w, 2 * w), 1)
    scan = jnp.where((key > c2) | (c2 >= w), 1.0, 0.0).astype(BF16)
    acc_scr[...] = jnp.zeros_like(acc_scr)
    run_scr[...] = jnp.zeros_like(run_scr)
    heads = range(SB_HEADS_PER_STEP)
    cols = [slice(hh * HEAD_DIM, (hh + 1) * HEAD_DIM) for hh in heads]
    qn = [_rms(q_ref[:, cols[hh]], qg_ref[...]).astype(BF16) for hh in heads]
    bias = [bias_ref[hg * SB_HEADS_PER_STEP + hh] for hh in heads]

    def block(kj, diagonal):
        start = pl.multiple_of(kj * t, t)
        scores = [_dot_nt(qn[hh], k_ref[pl.ds(start, t), cols[hh]].astype(BF16)) for hh in heads]
        zs, us, scanned = [], [], []
        for hh in heads:
            z = scores[hh] * (HEAD_DIM ** -0.5) + bias[hh]
            u = jnp.maximum(z, 0.0) + _softplus_neg_abs(z)
            if diagonal:
                u = jnp.where(causal, u, 0.0)
            halves = []
            for part in (u[:, :w], u[:, w:]):
                hi, lo = _split_bf16(part)
                halves.append(_dot(jnp.concatenate([hi, lo], axis=1), scan))
            zs.append(z)
            us.append(u)
            scanned.append(halves)
        weights = []
        for hh in heads:
            (left, right), run = scanned[hh], run_scr[hh]
            right_of_left = right[:, w:] + run
            s_left = us[hh][:, :w] + left[:, :w] + right_of_left
            s_right = us[hh][:, w:] + right[:, :w] + run
            e = jnp.exp(zs[hh] - jnp.concatenate([s_left, s_right], axis=1))
            if diagonal:
                e = jnp.where(causal, e, 0.0)
            weights.append(e.astype(BF16))
            run_scr[hh] = left[:, w:] + right_of_left
        for hh in heads:
            acc_scr[hh] += _dot(weights[hh], v_ref[pl.ds(start, t), cols[hh]].astype(BF16))

    block(qi, True)

    def body(i, carry):
        block(qi - 1 - i, False)
        return carry

    lax.fori_loop(0, qi, body, 0)
    for hh in heads:
        o_ref[:, cols[hh]] = acc_scr[hh].astype(o_ref.dtype)


def sb_prompt(proj, sk, q_gain, bias, layer, batch, seq):
    t = SB_TILE
    nq = seq // t
    hs = SB_HEADS_PER_STEP
    width = hs * HEAD_DIM
    groups = HEADS // hs
    return pl.pallas_call(
        _sb_prompt_kernel,
        grid_spec=pltpu.PrefetchScalarGridSpec(
            num_scalar_prefetch=1,
            grid=(batch, groups, nq),
            in_specs=[pl.BlockSpec((t, width), lambda b, g, qi, s: (b * nq + qi, COL_SQ * groups + g)),
                      pl.BlockSpec((None, seq, width), lambda b, g, qi, s: (layer, b, g)),
                      pl.BlockSpec((seq, width), lambda b, g, qi, s: (b, COL_SV * groups + g)),
                      pl.BlockSpec((None, 1, HEAD_DIM), lambda b, g, qi, s: (layer, 0, 0))],
            out_specs=pl.BlockSpec((t, width), lambda b, g, qi, s: (b * nq + qi, g)),
            scratch_shapes=[pltpu.VMEM((hs, t, HEAD_DIM), F32), pltpu.VMEM((hs, t, HEAD_DIM), F32)]),
        out_shape=jax.ShapeDtypeStruct((batch * seq, MIX_W), BF16),
        compiler_params=_params("parallel", "parallel", "parallel"),
        name="sb_prompt",
    )(bias, proj, sk, proj, q_gain.reshape(DEPTH, 1, HEAD_DIM))


def _decode_mix_kernel(lg_ref, p_ref, s_ref, cprev_ref, cos_ref, sin_ref, gn_ref, cw_ref, qg_ref, kg_ref,
                       ya_ref, yb_ref, sq_ref, sk_ref, s_out_ref, cn_ref):
    cos2 = cos_ref[...]
    sin2 = sin_ref[...]
    row = lax.broadcasted_iota(jnp.int32, (HEAD_DIM, HEAD_DIM), 0)
    col = lax.broadcasted_iota(jnp.int32, (HEAD_DIM, HEAD_DIM), 1)

    def stream(c, h):
        return p_ref[:, c * MIX_W + h * HEAD_DIM:c * MIX_W + (h + 1) * HEAD_DIM]

    def bf_round(x):
        return x.astype(BF16).astype(F32)

    for h in range(HEADS):
        cols = slice(h * HEAD_DIM, (h + 1) * HEAD_DIM)
        lg = lg_ref[h]
        q = bf_round(_rope(stream(COL_RQ, h), cos2, sin2))
        k = bf_round(_rope(stream(COL_RK, h), cos2, sin2) * (HEAD_DIM ** -0.5))
        v = bf_round(stream(COL_RV, h))
        s = s_ref[h]
        score = bf_round(jnp.sum(q * k, axis=-1, keepdims=True))
        inner = score * v
        q_rows = jnp.broadcast_to(q, (16, HEAD_DIM)).astype(BF16)
        cross = _dot(q_rows, s.astype(BF16))[0:1, :] * jnp.exp(lg * jnp.ones((1, 1), F32))
        k_col = jnp.sum(jnp.where(row == col, jnp.broadcast_to(k, (HEAD_DIM, HEAD_DIM)), 0.0),
                        axis=1, keepdims=True)
        s_out_ref[h] = jnp.exp(lg * jnp.ones((1, 1), F32)) * s + k_col * v
        ya_ref[:, cols] = _rms(inner + cross, gn_ref[h:h + 1, :]).astype(ya_ref.dtype)
        sq_ref[:, cols] = _rms(stream(COL_SQ, h), qg_ref[...])
        sk_ref[:, cols] = _rms(stream(COL_SK, h), kg_ref[...])

    ch = p_ref[:, COL_CH * MIX_W:(COL_CH + 1) * MIX_W]
    cb = p_ref[:, COL_CB * MIX_W:(COL_CB + 1) * MIX_W]
    cc = p_ref[:, COL_CC * MIX_W:(COL_CC + 1) * MIX_W]
    u = cc * ch
    cv = cprev_ref[0:1, :] * cw_ref[0:1, :] + cprev_ref[1:2, :] * cw_ref[1:2, :] + u * cw_ref[2:3, :]
    yb_ref[...] = (cb * cv).astype(yb_ref.dtype)
    cn_ref[0:1, :] = cprev_ref[1:2, :]
    cn_ref[1:2, :] = u


def decode_mix(proj_s, state_ret, state_conv, cos2, sin2, lg, gn, conv_w, q_gain, k_gain, layer, nb):
    def row_spec(width):
        return pl.BlockSpec((None, 1, width), lambda b, s: (b, 0, 0))

    def w_spec(*shape):
        return pl.BlockSpec((None,) + shape, lambda b, s: (layer,) + (0,) * len(shape))

    tab_spec = pl.BlockSpec((1, HEAD_DIM), lambda b, s: (0, 0))
    return pl.pallas_call(
        _decode_mix_kernel,
        grid_spec=pltpu.PrefetchScalarGridSpec(
            num_scalar_prefetch=1,
            grid=(nb,),
            in_specs=[row_spec(N_IN),
                      pl.BlockSpec((None, None, HEADS, HEAD_DIM, HEAD_DIM), lambda b, s: (layer, b, 0, 0, 0)),
                      pl.BlockSpec((None, None, 2, MIX_W), lambda b, s: (layer, b, 0, 0)),
                      tab_spec, tab_spec, w_spec(HEADS, HEAD_DIM), w_spec(3, MIX_W),
                      w_spec(1, HEAD_DIM), w_spec(1, HEAD_DIM)],
            out_specs=[row_spec(MIX_W), row_spec(MIX_W), row_spec(MIX_W), row_spec(MIX_W),
                       pl.BlockSpec((None, HEADS, HEAD_DIM, HEAD_DIM), lambda b, s: (b, 0, 0, 0)),
                       pl.BlockSpec((None, 2, MIX_W), lambda b, s: (b, 0, 0))]),
        out_shape=[jax.ShapeDtypeStruct((nb, 1, MIX_W), BF16),
                   jax.ShapeDtypeStruct((nb, 1, MIX_W), BF16),
                   jax.ShapeDtypeStruct((nb, 1, MIX_W), F32),
                   jax.ShapeDtypeStruct((nb, 1, MIX_W), F32),
                   jax.ShapeDtypeStruct((nb, HEADS, HEAD_DIM, HEAD_DIM), F32),
                   jax.ShapeDtypeStruct((nb, 2, MIX_W), F32)],
        compiler_params=_params("parallel"),
        name="decode_mix",
    )(lg, proj_s, state_ret, state_conv, cos2, sin2, gn, conv_w,
      q_gain.reshape(DEPTH, 1, HEAD_DIM), k_gain.reshape(DEPTH, 1, HEAD_DIM))


PAGES_PER_STEP = 16
WINDOW = PAGES_PER_STEP * PAGE_SIZE
SCAN_TOKENS = 1024
LANES = 128
SUB_ROWS = 16


def _sb_decode_kernel(pt_ref, q_ref, bias_ref, *refs, n_steps):
    g_n = PAGES_PER_STEP
    k_refs, v_refs = refs[:g_n], refs[g_n:2 * g_n]
    o_ref, qb_scr, scan_scr, acc_scr, run_scr = refs[2 * g_n:]
    s = pl.program_id(1)
    head_row = lax.broadcasted_iota(jnp.int32, (SUB_ROWS, WINDOW), 0)

    def head_window(page_refs, h):
        return jnp.concatenate(
            [r[pl.ds(h, PAGE_SIZE, stride=HEADS), :].astype(BF16) for r in page_refs], axis=0)

    @pl.when(s == 0)
    def _():
        acc_scr[...] = jnp.zeros_like(acc_scr)
        run_scr[...] = jnp.zeros_like(run_scr)
        q = q_ref[...]
        for h in range(HEADS):
            qb_scr[:, h * HEAD_DIM:(h + 1) * HEAD_DIM] = jnp.where(head_row[:, :HEAD_DIM] == h, q, 0.0).astype(BF16)
        r2 = lax.broadcasted_iota(jnp.int32, (SCAN_TOKENS, SCAN_TOKENS + LANES), 0)
        c2 = lax.broadcasted_iota(jnp.int32, (SCAN_TOKENS, SCAN_TOKENS + LANES), 1)
        scan_scr[...] = jnp.where((r2 > c2) | (c2 >= SCAN_TOKENS), 1.0, 0.0).astype(BF16)

    k_all = jnp.concatenate([head_window(k_refs, h) for h in range(HEADS)], axis=1)
    z = _dot_nt(qb_scr[...], k_all) * (HEAD_DIM ** -0.5) + bias_ref[...]
    sp = _softplus_neg_abs(z)
    log_b = jnp.minimum(z, 0.0) - sp
    log_1mb = -jnp.maximum(z, 0.0) - sp
    hi, lo = _split_bf16(log_1mb)
    after = run_scr[...]
    suffix = []
    for seg in reversed(range(WINDOW // SCAN_TOKENS)):
        cols = slice(seg * SCAN_TOKENS, (seg + 1) * SCAN_TOKENS)
        r = _dot(jnp.concatenate([hi[:, cols], lo[:, cols]], axis=0), scan_scr[...])
        r = r[:SUB_ROWS, :] + r[SUB_ROWS:, :]
        suffix.insert(0, r[:, :SCAN_TOKENS] + jnp.concatenate([after] * (SCAN_TOKENS // LANES), axis=1))
        after = after + r[:, SCAN_TOKENS:]
    run_scr[...] = after
    a = jnp.exp(log_b + jnp.concatenate(suffix, axis=1))

    a_all = jnp.concatenate([jnp.where(head_row == h, a, 0.0).astype(BF16) for h in range(HEADS)], axis=1)
    v_all = jnp.concatenate([head_window(v_refs, h) for h in range(HEADS)], axis=0)
    acc = acc_scr[...] + _dot(a_all, v_all)
    acc_scr[...] = acc

    @pl.when(s == n_steps - 1)
    def _():
        o_ref[...] = acc[0:HEADS, :]


def sb_decode(q_rows, bias_col, cache_k, cache_v, page_table, layer):
    nb, n_pages = page_table.shape
    assert n_pages % PAGES_PER_STEP == 0, (n_pages, PAGES_PER_STEP)
    n_steps = n_pages // PAGES_PER_STEP

    def page_spec(g):
        return pl.BlockSpec(
            (None, None, PAGE_SIZE * HEADS, HEAD_DIM),
            lambda b, s, pt: (layer, pt[b, n_pages - (s + 1) * PAGES_PER_STEP + g], 0, 0))

    pages = [page_spec(g) for g in range(PAGES_PER_STEP)]
    return pl.pallas_call(
        functools.partial(_sb_decode_kernel, n_steps=n_steps),
        grid_spec=pltpu.PrefetchScalarGridSpec(
            num_scalar_prefetch=1,
            grid=(nb, n_steps),
            in_specs=[pl.BlockSpec((None, SUB_ROWS, HEAD_DIM), lambda b, s, pt: (b, 0, 0)),
                      pl.BlockSpec((None, SUB_ROWS, 1), lambda b, s, pt: (layer, 0, 0))] + pages + pages,
            out_specs=pl.BlockSpec((None, HEADS, HEAD_DIM), lambda b, s, pt: (b, 0, 0)),
            scratch_shapes=[pltpu.VMEM((SUB_ROWS, HEADS * HEAD_DIM), BF16),
                            pltpu.VMEM((SCAN_TOKENS, SCAN_TOKENS + LANES), BF16),
                            pltpu.VMEM((SUB_ROWS, HEAD_DIM), F32),
                            pltpu.VMEM((SUB_ROWS, LANES), F32)]),
        out_shape=jax.ShapeDtypeStruct((nb, HEADS, HEAD_DIM), F32),
        compiler_params=_params("parallel", "arbitrary"),
        name="sb_decode",
    )(page_table, q_rows, bias_col, *([cache_k] * PAGES_PER_STEP), *([cache_v] * PAGES_PER_STEP))


def _rope_tables(pos):
    half = HEAD_DIM // 2
    inv = ROPE_THETA ** (-jnp.arange(half, dtype=F32) / half)
    ang = pos.astype(F32)[:, None] * inv[None, :]
    cos, sin = jnp.cos(ang), jnp.sin(ang)
    return jnp.concatenate([cos, cos], axis=-1), jnp.concatenate([-sin, sin], axis=-1)


def kernel(x_prompt, x_sample, cache_sb_k, cache_sb_v, state_ret, state_conv, page_table, p_prompt, p_sample,
           ffn1_norm, ffn1_w_gu, ffn1_w_down, mix_norm, w_in, ret_gn, conv_w, sb_q_norm, sb_k_norm, sb_bias,
           w_branch_ret, w_branch_conv, w_branch_sb, w_out, ffn2_norm, ffn2_w_gu, ffn2_w_down,
           ple_norm, w_ple_gate, w_ple_up):
    batch, seq, _ = x_prompt.shape
    nb = x_sample.shape[0]
    m_p = batch * seq
    m_s = 16

    lg =jnp.log1p(-jnp.exp2(-5.0 - jnp.arange(HEADS, dtype=F32)))
    cos_p, sin_p = _rope_tables(jnp.arange(seq, dtype=jnp.int32))
    past_len = page_table.shape[1] * PAGE_SIZE
    cos_s, sin_s = _rope_tables(jnp.full((1,), past_len, jnp.int32))
    tp = lambda t: min(t, m_p)
    page_rows = cache_sb_k.shape[:2] + (PAGE_SIZE * HEADS, HEAD_DIM)
    cache_sb_k, cache_sb_v = cache_sb_k.reshape(page_rows), cache_sb_v.reshape(page_rows)
    row_pad = ((0, 0), (0, SUB_ROWS - HEADS), (0, 0))
    bias_cols = jnp.pad(sb_bias.reshape(DEPTH, HEADS, 1), row_pad)

    xp = x_prompt.reshape(m_p, D_MODEL)
    xs = jnp.pad(x_sample.reshape(nb, D_MODEL), ((0, m_s - nb), (0, 0)))
    pp = p_prompt.reshape(DEPTH, m_p, P_DIM)
    ps = jnp.pad(p_sample.reshape(DEPTH, nb, P_DIM), ((0, 0), (0, m_s - nb), (0, 0)))

    def ffn(stream, w_gu, w_down, next_gain, i):
        xp, xs, gp, gs, qp, qs = stream
        ap, a_s = dense("swiglu_mm", _swiglu_body, [(gp, None)], [(gs, None)],
                        [(w_gu, 0), (w_gu, D_FF)], i, D_FF, BF16, norm_in=(qp, qs))
        return dense("ffn_down", functools.partial(_residual_body, scale=0.5),
                     [(ap, None), (xp, 0)], [(a_s, None), (xs, 0)], [(w_down, 0)], i, D_MODEL, F32,
                     norm_out=(next_gain, i))

    stream = (xp, xs) + tuple(prenorm(xp, xs, ffn1_norm, 0, tp(512)))
    outs = [[] for _ in range(6)]
    kv_p = None
    for i in range(DEPTH):
        stream = ffn(stream, ffn1_w_gu, ffn1_w_down, mix_norm, i)
        xp, xs, gp, gs, qp, qs = stream
        proj, proj_s = dense("in_proj", _proj_body, [(gp, None)], [(gs, None)], [(w_in, 0)], i, N_IN, F32,
                             norm_in=(qp, qs))

        ya, s_p = retention_prompt(proj, cos_p, sin_p, lg, ret_gn, i, batch, seq)
        yb, c_p = conv_prompt(proj, conv_w, i, batch, seq)
        kv_p = new_kv(proj, sb_k_norm, i, kv_p, tp(512))
        yc = sb_prompt(proj, kv_p[0], sb_q_norm, sb_bias[i], i, batch, seq)

        ya_s, yb_s, sq_s, sk_s, s_s, c_s = decode_mix(
            proj_s[:nb].reshape(nb, 1, N_IN), state_ret, state_conv, cos_s, sin_s, lg, ret_gn, conv_w,
            sb_q_norm, sb_k_norm, i, nb)
        q_rows = jnp.pad(sq_s.reshape(nb, HEADS, HEAD_DIM), row_pad)
        yc_s = sb_decode(q_rows, bias_cols, cache_sb_k, cache_sb_v, page_table, i)
        yc_s = yc_s.reshape(nb, MIX_W).astype(BF16)
        pad_rows = lambda y: jnp.pad(y.reshape(nb, MIX_W), ((0, m_s - nb), (0, 0)))
        sv_s = proj_s[:nb, COL_SV * MIX_W:(COL_SV + 1) * MIX_W]

        def branch_rows(ya, yb, yc, proj):
            return [(ya, None), (yb, None), (yc, None),
                    (proj, COL_GA * MIX_W), (proj, COL_GB * MIX_W), (proj, COL_GC * MIX_W)]

        mp, ms = dense("merge_branches", _merge_body, branch_rows(ya, yb, yc, proj),
                       branch_rows(pad_rows(ya_s), pad_rows(yb_s), pad_rows(yc_s), proj_s),
                       [(w_branch_ret, 0), (w_branch_conv, 0), (w_branch_sb, 0)], i, D_MODEL, BF16)
        stream = dense("out_proj", functools.partial(_residual_body, scale=1.0),
                       [(mp, None), (xp, 0)], [(ms, None), (xs, 0)], [(w_out, 0)], i, D_MODEL, F32,
                       norm_out=(ffn2_norm, i))
        xp, xs, gp, gs, qp, qs = ffn(stream, ffn2_w_gu, ffn2_w_down, ple_norm, i)
        stream = dense("ple_update", _ple_body, [(gp, None), (pp[i], None), (xp, 0)],
                       [(gs, None), (ps[i], None), (xs, 0)], [(w_ple_gate, 0), (w_ple_up, 0)],
                       i, D_MODEL, F32, norm_in=(qp, qs),
                       norm_out=(ffn1_norm, i + 1) if i + 1 < DEPTH else None)
        xp, xs = stream[:2]

        for lst, val in zip(outs, (sk_s.reshape(nb, 1, HEADS, HEAD_DIM),
                                   sv_s.reshape(nb, 1, HEADS, HEAD_DIM),
                                   s_p, s_s, c_p, c_s)):
            lst.append(val)

    new_k_p, new_v_p = (a.reshape(DEPTH, batch, seq, HEADS, HEAD_DIM) for a in kv_p)
    return ((xp.reshape(batch, seq, D_MODEL), xs[:nb].reshape(nb, 1, D_MODEL), new_k_p, new_v_p)
            + tuple(jnp.stack(o) for o in outs))
```

```python
import functools

import jax
import jax.numpy as jnp
from jax import lax
from jax.experimental import pallas as pl
from jax.experimental.pallas import tpu as pltpu

F32 = jnp.float32
BF16 = jnp.bfloat16

D_MODEL = 2048
DEPTH = 2
PAGE_SIZE = 128
HEADS = 8
HEAD_DIM = 128
MIX_W = HEADS * HEAD_DIM
CHUNK = 128
D_FF = 5632
P_DIM = 256
ROPE_THETA = 10000.0
EPS = 1e-6
N_IN = 9 * MIX_W + 3 * D_MODEL
COL_RQ, COL_RK, COL_RV, COL_CH, COL_CB, COL_CC, COL_SQ, COL_SK, COL_SV = range(9)
COL_GA, COL_GB, COL_GC = 9, 11, 13

VMEM_LIMIT_BYTES = 56 * 1024 * 1024


def _params(*sem):
    return pltpu.CompilerParams(dimension_semantics=sem, vmem_limit_bytes=VMEM_LIMIT_BYTES)


def _dot(a, b):
    return jnp.dot(a, b, preferred_element_type=F32)


def _dot_nt(a, b):
    return lax.dot_general(a, b, (((1,), (1,)), ((), ())), preferred_element_type=F32)


def _dot_tn(a, b):
    return lax.dot_general(a, b, (((0,), (0,)), ((), ())), preferred_element_type=F32)


def _rms(x, g):
    return x * lax.rsqrt(jnp.mean(x * x, axis=-1, keepdims=True) + EPS) * g


def _sigmoid(x):
    return jax.nn.sigmoid(x)


def _row_scale(sumsq_parts):
    return lax.rsqrt(jnp.sum(sumsq_parts, axis=0) * (1.0 / D_MODEL) + EPS)


def _scaled(acc, r):
    return acc * jnp.concatenate([r] * (acc.shape[1] // LANES), axis=1)


def _gain_and_sumsq(x, gain, gained_ref, sumsq_ref):
    gained_ref[...] = (x * gain).astype(gained_ref.dtype)
    sumsq_ref[...] = jnp.broadcast_to(jnp.sum(x * x, axis=1, keepdims=True), sumsq_ref.shape)


def _prenorm_kernel(xp_ref, xs_ref, g_ref, gp_ref, gs_ref, qp_ref, qs_ref):
    _gain_and_sumsq(xp_ref[...], g_ref[...], gp_ref, qp_ref)

    @pl.when(pl.program_id(0) == 0)
    def _():
        _gain_and_sumsq(xs_ref[...], g_ref[...], gs_ref, qs_ref)


def prenorm(xp, xs, g, layer, tm):
    m, d = xp.shape
    ms = xs.shape[0]
    return pl.pallas_call(
        _prenorm_kernel,
        grid=(m // tm,),
        in_specs=[pl.BlockSpec((tm, d), lambda i: (i, 0)),
                  pl.BlockSpec((ms, d), lambda i: (0, 0)),
                  pl.BlockSpec((None, 1, d), lambda i: (layer, 0, 0))],
        out_specs=[pl.BlockSpec((tm, d), lambda i: (i, 0)),
                   pl.BlockSpec((ms, d), lambda i: (0, 0)),
                   pl.BlockSpec((None, tm, LANES), lambda i: (0, i, 0)),
                   pl.BlockSpec((None, ms, LANES), lambda i: (0, 0, 0))],
        out_shape=[jax.ShapeDtypeStruct((m, d), BF16), jax.ShapeDtypeStruct((ms, d), BF16),
                   jax.ShapeDtypeStruct((1, m, LANES), F32), jax.ShapeDtypeStruct((1, ms, LANES), F32)],
        compiler_params=_params("arbitrary"),
        name="prenorm",
    )(xp, xs, g.reshape(DEPTH, 1, d))


def _dense_kernel(*refs, body, n_rows, n_w, norm_in, norm_out):
    refs = iter(refs)
    take = lambda n: [next(refs) for _ in range(n)]
    rows_p, rows_s = take(n_rows), take(n_rows)
    sumsq_in = take(2) if norm_in else None
    first_refs, next_refs = take(n_w), take(n_w)
    gain_ref = take(1)[0] if norm_out else None
    op_ref, os_ref = take(2)
    norm_refs = take(4) if norm_out else None
    w_scr = take(n_w)
    scale_scr = take(2) if norm_in else None
    j, i = pl.program_id(0), pl.program_id(1)
    cur = j % 2

    @pl.when((j == 0) & (i == 0))
    def _():
        for first, scr in zip(first_refs, w_scr):
            scr[0] = first[...].astype(BF16)
        if norm_in:
            scale_scr[1][...] = _row_scale(sumsq_in[1][...])

    for nxt, scr in zip(next_refs, w_scr):
        rows = nxt.shape[0]
        scr[1 - cur, pl.ds(pl.multiple_of(i * rows, rows), rows), :] = nxt[...].astype(BF16)

    if norm_in:
        @pl.when(j == 0)
        def _():
            scale_scr[0][i] = _row_scale(sumsq_in[0][...])

    def emit(rows, scale, o_ref, gained_ref, sumsq_ref):
        out = body([r[...] for r in rows], [scr[cur] for scr in w_scr], scale)
        o_ref[...] = out.astype(o_ref.dtype)
        if norm_out:
            _gain_and_sumsq(out, gain_ref[...], gained_ref, sumsq_ref)

    @pl.when(i == 0)
    def _():
        emit(rows_s, scale_scr[1][...] if norm_in else None, os_ref,
             norm_refs[1] if norm_out else None, norm_refs[3] if norm_out else None)

    emit(rows_p, scale_scr[0][i] if norm_in else None, op_ref,
         norm_refs[0] if norm_out else None, norm_refs[2] if norm_out else None)


DENSE_TILES = {
    "swiglu_mm": (1024, 512), "ffn_down": (512, 512), "in_proj": (1024, 1024),
    "merge_branches": (512, 1024), "out_proj": (512, 1024), "ple_update": (512, 1024),
}


def dense(name, body, rows_p, rows_s, weights, layer, n_out, out_dtype, norm_in=None, norm_out=None):
    tm, tn = DENSE_TILES[name]
    rows_p = [(a, c if c is None else c // tn) for a, c in rows_p]
    rows_s = [(a, c if c is None else c // tn) for a, c in rows_s]
    weights = [(a, c // tn) for a, c in weights]
    m_p, m_s = rows_p[0][0].shape[0], rows_s[0][0].shape[0]
    tm = min(tm, m_p)

    def row_specs(rows, t, tiled):
        specs = []
        for arr, col in rows:
            if col is None:
                specs.append(pl.BlockSpec((t, arr.shape[1]), (lambda j, i: (i, 0)) if tiled else (lambda j, i: (0, 0))))
            elif tiled:
                specs.append(pl.BlockSpec((t, tn), lambda j, i, col=col: (i, col + j)))
            else:
                specs.append(pl.BlockSpec((t, tn), lambda j, i, col=col: (0, col + j)))
        return specs

    n_j, n_i = n_out // tn, m_p // tm
    first_specs = [pl.BlockSpec((None, w.shape[1], tn), lambda j, i, col=col: (layer, 0, col),
                                pipeline_mode=pl.Buffered(1)) for w, col in weights]
    for w, _ in weights:
        assert w.shape[1] % (n_i * SUB_ROWS) == 0, (name, w.shape, n_i)
    next_specs = [pl.BlockSpec((None, w.shape[1] // n_i, tn),
                               lambda j, i, col=col: (layer, i, col + jnp.minimum(j + 1, n_j - 1)))
                  for w, col in weights]
    tile_p = pl.BlockSpec((tm, tn), lambda j, i: (i, j))
    tile_s = pl.BlockSpec((m_s, tn), lambda j, i: (0, j))
    in_specs = row_specs(rows_p, tm, True) + row_specs(rows_s, m_s, False)
    operands = [a for a, _ in rows_p] + [a for a, _ in rows_s]
    out_specs, out_shape = [tile_p, tile_s], [jax.ShapeDtypeStruct((m_p, n_out), out_dtype),
                                              jax.ShapeDtypeStruct((m_s, n_out), out_dtype)]
    scratch = [pltpu.VMEM((2, w.shape[1], tn), BF16) for w, _ in weights]
    if norm_in is not None:
        parts = norm_in[0].shape[0]
        in_specs += [pl.BlockSpec((parts, tm, LANES), lambda j, i: (0, jnp.where(j == 0, i, n_i - 1), 0)),
                     pl.BlockSpec((parts, m_s, LANES), lambda j, i: (0, 0, 0))]
        operands += list(norm_in)
        scratch += [pltpu.VMEM((n_i, tm, LANES), F32), pltpu.VMEM((m_s, LANES), F32)]
    in_specs += first_specs + next_specs
    operands += [w for w, _ in weights] * 2
    if norm_out is not None:
        gain, gain_layer = norm_out
        in_specs.append(pl.BlockSpec((None, 1, tn), lambda j, i: (gain_layer, 0, j)))
        operands.append(gain.reshape(DEPTH, 1, n_out))
        out_specs += [tile_p, tile_s, pl.BlockSpec((None, tm, LANES), lambda j, i: (j, i, 0)),
                      pl.BlockSpec((None, m_s, LANES), lambda j, i: (j, 0, 0))]
        out_shape += [jax.ShapeDtypeStruct((m_p, n_out), BF16), jax.ShapeDtypeStruct((m_s, n_out), BF16),
                      jax.ShapeDtypeStruct((n_j, m_p, LANES), F32), jax.ShapeDtypeStruct((n_j, m_s, LANES), F32)]
    return pl.pallas_call(
        functools.partial(_dense_kernel, body=body, n_rows=len(rows_p), n_w=len(weights),
                          norm_in=norm_in is not None, norm_out=norm_out is not None),
        grid=(n_j, n_i),
        in_specs=in_specs,
        out_specs=out_specs,
        out_shape=out_shape,
        scratch_shapes=scratch,
        compiler_params=_params("arbitrary", "arbitrary"),
        name=name,
    )(*operands)


def _swiglu_body(rows, ws, r):
    (h,), (wg, wu) = rows, ws
    g = _scaled(_dot(h, wg), r)
    return g * _sigmoid(g) * _scaled(_dot(h, wu), r)


def _residual_body(rows, ws, r, *, scale):
    (a, x), (w,) = rows, ws
    return x + scale * _dot(a, w)


def _proj_body(rows, ws, r):
    return _scaled(_dot(rows[0], ws[0]), r)


def _merge_body(rows, ws, r):
    (ya, yb, yc, ga, gb, gc), (wr, wc, wsb) = rows, ws
    return _sigmoid(ga) * _dot(ya, wr) + _sigmoid(gb) * _dot(yb, wc) + _sigmoid(gc) * _dot(yc, wsb)


def _ple_body(rows, ws, r):
    (h, p, x), (wg, wu) = rows, ws
    return x + _sigmoid(_scaled(_dot(h, wg), r)) * _dot(p.astype(BF16), wu)


def _rope(x, cos2, sin2):
    return x * cos2 + pltpu.roll(x, HEAD_DIM // 2, 1) * sin2


def _softplus_neg_abs(z):
    neg_abs = lax.bitcast_convert_type(
        lax.bitcast_convert_type(z, jnp.uint32) | jnp.uint32(0x80000000), F32)
    return jnp.log(1.0 + jnp.exp(neg_abs))


def _split_bf16(x):
    hi = x.astype(BF16)
    return hi, (x - hi.astype(F32)).astype(BF16)


def _retention_kernel(lg_ref, q_ref, k_ref, v_ref, cos_ref, sin_ref, gn_ref, ya_ref, s_out_ref, s_scr,
                      *, n_chunks):
    c = pl.program_id(1)

    @pl.when(c == 0)
    def _():
        s_scr[...] = jnp.zeros_like(s_scr)

    cos2 = cos_ref[...]
    sin2 = sin_ref[...]
    ii = lax.broadcasted_iota(jnp.int32, (CHUNK, CHUNK), 0)
    jj = lax.broadcasted_iota(jnp.int32, (CHUNK, CHUNK), 1)
    diff = (ii - jj).astype(F32)
    pos = lax.broadcasted_iota(jnp.int32, (CHUNK, 1), 0).astype(F32)
    heads = range(HEADS)
    cols = [slice(h * HEAD_DIM, (h + 1) * HEAD_DIM) for h in heads]
    lg = [lg_ref[h] for h in heads]
    qb, kb, kd, vb = [], [], [], []
    for h in heads:
        k = _rope(k_ref[:, cols[h]], cos2, sin2) * (HEAD_DIM ** -0.5)
        qb.append(_rope(q_ref[:, cols[h]], cos2, sin2).astype(BF16))
        kb.append(k.astype(BF16))
        kd.append((k * jnp.exp(lg[h] * (CHUNK - 1.0 - pos))).astype(BF16))
        vb.append(v_ref[:, cols[h]].astype(BF16))
    state = [s_scr[h] for h in heads]
    scores = [_dot_nt(qb[h], kb[h]) for h in heads]
    cross = [_dot(qb[h], state[h].astype(BF16)) for h in heads]
    update = [_dot_tn(kd[h], vb[h]) for h in heads]
    inner = []
    for h in heads:
        dmask = jnp.where(diff >= 0, jnp.exp(lg[h] * jnp.maximum(diff, 0.0)), 0.0)
        inner.append(_dot((scores[h] * dmask).astype(BF16), vb[h]))
    for h in heads:
        o = inner[h] + cross[h] * jnp.exp(lg[h] * (pos + 1.0))
        ya_ref[:, cols[h]] = _rms(o, gn_ref[h:h + 1, :]).astype(ya_ref.dtype)
        s_scr[h] = jnp.exp(lg[h] * CHUNK) * state[h] + update[h]

    @pl.when(c == n_chunks - 1)
    def _():
        s_out_ref[...] = s_scr[...]


def retention_prompt(proj, cos2, sin2, lg, gn, layer, batch, seq):
    n_chunks = seq // CHUNK
    m = batch * seq

    def col_spec(col):
        return pl.BlockSpec((CHUNK, MIX_W), lambda b, c, lg_ref: (b * n_chunks + c, col))

    tab_spec = pl.BlockSpec((CHUNK, HEAD_DIM), lambda b, c, lg_ref: (c, 0))
    return pl.pallas_call(
        functools.partial(_retention_kernel, n_chunks=n_chunks),
        grid_spec=pltpu.PrefetchScalarGridSpec(
            num_scalar_prefetch=1,
            grid=(batch, n_chunks),
            in_specs=[col_spec(COL_RQ), col_spec(COL_RK), col_spec(COL_RV), tab_spec, tab_spec,
                      pl.BlockSpec((None, HEADS, HEAD_DIM), lambda b, c, lg_ref: (layer, 0, 0))],
            out_specs=[pl.BlockSpec((CHUNK, MIX_W), lambda b, c, lg_ref: (b * n_chunks + c, 0)),
                       pl.BlockSpec((None, HEADS, HEAD_DIM, HEAD_DIM), lambda b, c, lg_ref: (b, 0, 0, 0))],
            scratch_shapes=[pltpu.VMEM((HEADS, HEAD_DIM, HEAD_DIM), F32)]),
        out_shape=[jax.ShapeDtypeStruct((m, MIX_W), BF16),
                   jax.ShapeDtypeStruct((batch, HEADS, HEAD_DIM, HEAD_DIM), F32)],
        compiler_params=_params("parallel", "arbitrary"),
        name="retention_prompt",
    )(lg, proj, proj, proj, cos2, sin2, gn)


def _conv_kernel(ch_ref, cb_ref, cc_ref, w_ref, yb_ref, cn_ref):
    u = cc_ref[...] * ch_ref[...]
    seq = u.shape[0]
    t = lax.broadcasted_iota(jnp.int32, u.shape, 0)
    u1 = jnp.where(t >= 1, pltpu.roll(u, 1, 0), 0.0)
    u2 = jnp.where(t >= 2, pltpu.roll(u, 2, 0), 0.0)
    cv = u2 * w_ref[0:1, :] + u1 * w_ref[1:2, :] + u * w_ref[2:3, :]
    yb_ref[...] = (cb_ref[...] * cv).astype(yb_ref.dtype)
    cn_ref[...] = cc_ref[seq - 2:seq, :] * ch_ref[seq - 2:seq, :]


def conv_prompt(proj, conv_w, layer, batch, seq):
    tw = 256
    nw = MIX_W // tw

    def col_spec(col):
        return pl.BlockSpec((seq, tw), lambda b, j: (b, col * nw + j))

    return pl.pallas_call(
        _conv_kernel,
        grid=(batch, nw),
        in_specs=[col_spec(COL_CH), col_spec(COL_CB), col_spec(COL_CC),
                  pl.BlockSpec((None, 3, tw), lambda b, j: (layer, 0, j))],
        out_specs=[pl.BlockSpec((seq, tw), lambda b, j: (b, j)),
                   pl.BlockSpec((None, 2, tw), lambda b, j: (b, 0, j))],
        out_shape=[jax.ShapeDtypeStruct((batch * seq, MIX_W), BF16),
                   jax.ShapeDtypeStruct((batch, 2, MIX_W), F32)],
        compiler_params=_params("parallel", "parallel"),
        name="conv_prompt",
    )(proj, proj, proj, conv_w)


def _new_kv_kernel(k_ref, v_ref, g_ref, *refs, layer, first):
    ok_ref, ov_ref = refs[-2:]

    def write_layer():
        g = g_ref[...]
        for h in range(HEADS):
            cols = slice(h * HEAD_DIM, (h + 1) * HEAD_DIM)
            ok_ref[:, cols] = _rms(k_ref[:, cols], g)
        ov_ref[...] = v_ref[...]

    if first:
        pl.when(pl.program_id(0) == layer)(write_layer)

        @pl.when(pl.program_id(0) != layer)
        def _():
            ok_ref[...] = jnp.zeros_like(ok_ref)
            ov_ref[...] = jnp.zeros_like(ov_ref)
    else:
        write_layer()


def new_kv(proj, g, layer, prev, tm):
    m = proj.shape[0]
    n_i = m // tm
    first = prev is None
    if first:
        grid = (DEPTH, n_i)
        row = lambda l, i: jnp.where(l == layer, i, 0)
        in_specs = [pl.BlockSpec((tm, MIX_W), lambda l, i: (row(l, i), COL_SK)),
                    pl.BlockSpec((tm, MIX_W), lambda l, i: (row(l, i), COL_SV)),
                    pl.BlockSpec((None, 1, HEAD_DIM), lambda l, i: (layer, 0, 0))]
        out_spec = pl.BlockSpec((None, tm, MIX_W), lambda l, i: (l, i, 0))
        operands, aliases, sem = [proj, proj, g.reshape(DEPTH, 1, HEAD_DIM)], {}, ("arbitrary", "arbitrary")
    else:
        grid = (n_i,)
        in_specs = [pl.BlockSpec((tm, MIX_W), lambda i: (i, COL_SK)),
                    pl.BlockSpec((tm, MIX_W), lambda i: (i, COL_SV)),
                    pl.BlockSpec((None, 1, HEAD_DIM), lambda i: (layer, 0, 0)),
                    pl.BlockSpec(memory_space=pl.ANY), pl.BlockSpec(memory_space=pl.ANY)]
        out_spec = pl.BlockSpec((None, tm, MIX_W), lambda i: (layer, i, 0))
        operands = [proj, proj, g.reshape(DEPTH, 1, HEAD_DIM), prev[0], prev[1]]
        aliases, sem = {3: 0, 4: 1}, ("arbitrary",)
    return pl.pallas_call(
        functools.partial(_new_kv_kernel, layer=layer, first=first),
        grid=grid,
        in_specs=in_specs,
        out_specs=[out_spec, out_spec],
        out_shape=[jax.ShapeDtypeStruct((DEPTH, m, MIX_W), F32)] * 2,
        input_output_aliases=aliases,
        compiler_params=_params(*sem),
        name="new_kv",
    )(*operands)


SB_TILE = 256


SB_HEADS_PER_STEP = 8


def _sb_prompt_kernel(bias_ref, q_ref, k_ref, v_ref, qg_ref, o_ref, acc_scr, run_scr):
    t = SB_TILE
    w = HEAD_DIM
    hg = pl.program_id(1)
    qi = pl.program_id(2)
    row = lax.broadcasted_iota(jnp.int32, (t, t), 0)
    col = lax.broadcasted_iota(jnp.int32, (t, t), 1)
    causal = col < row
    key = jnp.bitwise_and(lax.broadcasted_iota(jnp.int32, (2 * w, 2 * w), 0), w - 1)
    c2 = lax.broadcasted_iota(jnp.int32, (2 * w, 2 * w), 1)
    scan = jnp.where((key > c2) | (c2 >= w), 1.0, 0.0).astype(BF16)
    acc_scr[...] = jnp.zeros_like(acc_scr)
    run_scr[...] = jnp.zeros_like(run_scr)
    heads = range(SB_HEADS_PER_STEP)
    cols = [slice(hh * HEAD_DIM, (hh + 1) * HEAD_DIM) for hh in heads]
    qn = [_rms(q_ref[:, cols[hh]], qg_ref[...]).astype(BF16) for hh in heads]
    bias = [bias_ref[hg * SB_HEADS_PER_STEP + hh] for hh in heads]

    def block(kj, diagonal):
        start = pl.multiple_of(kj * t, t)
        scores = [_dot_nt(qn[hh], k_ref[pl.ds(start, t), cols[hh]].astype(BF16)) for hh in heads]
        zs, us, scanned = [], [], []
        for hh in heads:
            z = scores[hh] * (HEAD_DIM ** -0.5) + bias[hh]
            u = jnp.maximum(z, 0.0) + _softplus_neg_abs(z)
            if diagonal:
                u = jnp.where(causal, u, 0.0)
            halves = []
            for part in (u[:, :w], u[:, w:]):
                hi, lo = _split_bf16(part)
                halves.append(_dot(jnp.concatenate([hi, lo], axis=1), scan))
            zs.append(z)
            us.append(u)
            scanned.append(halves)
        weights = []
        for hh in heads:
            (left, right), run = scanned[hh], run_scr[hh]
            right_of_left = right[:, w:] + run
            s_left = us[hh][:, :w] + left[:, :w] + right_of_left
            s_right = us[hh][:, w:] + right[:, :w] + run
            e = jnp.exp(zs[hh] - jnp.concatenate([s_left, s_right], axis=1))
            if diagonal:
                e = jnp.where(causal, e, 0.0)
            weights.append(e.astype(BF16))
            run_scr[hh] = left[:, w:] + right_of_left
        for hh in heads:
            acc_scr[hh] += _dot(weights[hh], v_ref[pl.ds(start, t), cols[hh]].astype(BF16))

    block(qi, True)

    def body(i, carry):
        block(qi - 1 - i, False)
        return carry

    lax.fori_loop(0, qi, body, 0)
    for hh in heads:
        o_ref[:, cols[hh]] = acc_scr[hh].astype(o_ref.dtype)


def sb_prompt(proj, sk, q_gain, bias, layer, batch, seq):
    t = SB_TILE
    nq = seq // t
    hs = SB_HEADS_PER_STEP
    width = hs * HEAD_DIM
    groups = HEADS // hs
    return pl.pallas_call(
        _sb_prompt_kernel,
        grid_spec=pltpu.PrefetchScalarGridSpec(
            num_scalar_prefetch=1,
            grid=(batch, groups, nq),
            in_specs=[pl.BlockSpec((t, width), lambda b, g, qi, s: (b * nq + qi, COL_SQ * groups + g)),
                      pl.BlockSpec((None, seq, width), lambda b, g, qi, s: (layer, b, g)),
                      pl.BlockSpec((seq, width), lambda b, g, qi, s: (b, COL_SV * groups + g)),
                      pl.BlockSpec((None, 1, HEAD_DIM), lambda b, g, qi, s: (layer, 0, 0))],
            out_specs=pl.BlockSpec((t, width), lambda b, g, qi, s: (b * nq + qi, g)),
            scratch_shapes=[pltpu.VMEM((hs, t, HEAD_DIM), F32), pltpu.VMEM((hs, t, HEAD_DIM), F32)]),
        out_shape=jax.ShapeDtypeStruct((batch * seq, MIX_W), BF16),
        compiler_params=_params("parallel", "parallel", "parallel"),
        name="sb_prompt",
    )(bias, proj, sk, proj, q_gain.reshape(DEPTH, 1, HEAD_DIM))


def _decode_mix_kernel(lg_ref, p_ref, s_ref, cprev_ref, cos_ref, sin_ref, gn_ref, cw_ref, qg_ref, kg_ref,
                       ya_ref, yb_ref, sq_ref, sk_ref, s_out_ref, cn_ref):
    cos2 = cos_ref[...]
    sin2 = sin_ref[...]
    row = lax.broadcasted_iota(jnp.int32, (HEAD_DIM, HEAD_DIM), 0)
    col = lax.broadcasted_iota(jnp.int32, (HEAD_DIM, HEAD_DIM), 1)

    def stream(c, h):
        return p_ref[:, c * MIX_W + h * HEAD_DIM:c * MIX_W + (h + 1) * HEAD_DIM]

    def bf_round(x):
        return x.astype(BF16).astype(F32)

    for h in range(HEADS):
        cols = slice(h * HEAD_DIM, (h + 1) * HEAD_DIM)
        lg = lg_ref[h]
        q = bf_round(_rope(stream(COL_RQ, h), cos2, sin2))
        k = bf_round(_rope(stream(COL_RK, h), cos2, sin2) * (HEAD_DIM ** -0.5))
        v = bf_round(stream(COL_RV, h))
        s = s_ref[h]
        score = bf_round(jnp.sum(q * k, axis=-1, keepdims=True))
        inner = score * v
        q_rows = jnp.broadcast_to(q, (16, HEAD_DIM)).astype(BF16)
        cross = _dot(q_rows, s.astype(BF16))[0:1, :] * jnp.exp(lg * jnp.ones((1, 1), F32))
        k_col = jnp.sum(jnp.where(row == col, jnp.broadcast_to(k, (HEAD_DIM, HEAD_DIM)), 0.0),
                        axis=1, keepdims=True)
        s_out_ref[h] = jnp.exp(lg * jnp.ones((1, 1), F32)) * s + k_col * v
        ya_ref[:, cols] = _rms(inner + cross, gn_ref[h:h + 1, :]).astype(ya_ref.dtype)
        sq_ref[:, cols] = _rms(stream(COL_SQ, h), qg_ref[...])
        sk_ref[:, cols] = _rms(stream(COL_SK, h), kg_ref[...])

    ch = p_ref[:, COL_CH * MIX_W:(COL_CH + 1) * MIX_W]
    cb = p_ref[:, COL_CB * MIX_W:(COL_CB + 1) * MIX_W]
    cc = p_ref[:, COL_CC * MIX_W:(COL_CC + 1) * MIX_W]
    u = cc * ch
    cv = cprev_ref[0:1, :] * cw_ref[0:1, :] + cprev_ref[1:2, :] * cw_ref[1:2, :] + u * cw_ref[2:3, :]
    yb_ref[...] = (cb * cv).astype(yb_ref.dtype)
    cn_ref[0:1, :] = cprev_ref[1:2, :]
    cn_ref[1:2, :] = u


def decode_mix(proj_s, state_ret, state_conv, cos2, sin2, lg, gn, conv_w, q_gain, k_gain, layer, nb):
    def row_spec(width):
        return pl.BlockSpec((None, 1, width), lambda b, s: (b, 0, 0))

    def w_spec(*shape):
        return pl.BlockSpec((None,) + shape, lambda b, s: (layer,) + (0,) * len(shape))

    tab_spec = pl.BlockSpec((1, HEAD_DIM), lambda b, s: (0, 0))
    return pl.pallas_call(
        _decode_mix_kernel,
        grid_spec=pltpu.PrefetchScalarGridSpec(
            num_scalar_prefetch=1,
            grid=(nb,),
            in_specs=[row_spec(N_IN),
                      pl.BlockSpec((None, None, HEADS, HEAD_DIM, HEAD_DIM), lambda b, s: (layer, b, 0, 0, 0)),
                      pl.BlockSpec((None, None, 2, MIX_W), lambda b, s: (layer, b, 0, 0)),
                      tab_spec, tab_spec, w_spec(HEADS, HEAD_DIM), w_spec(3, MIX_W),
                      w_spec(1, HEAD_DIM), w_spec(1, HEAD_DIM)],
            out_specs=[row_spec(MIX_W), row_spec(MIX_W), row_spec(MIX_W), row_spec(MIX_W),
                       pl.BlockSpec((None, HEADS, HEAD_DIM, HEAD_DIM), lambda b, s: (b, 0, 0, 0)),
                       pl.BlockSpec((None, 2, MIX_W), lambda b, s: (b, 0, 0))]),
        out_shape=[jax.ShapeDtypeStruct((nb, 1, MIX_W), BF16),
                   jax.ShapeDtypeStruct((nb, 1, MIX_W), BF16),
                   jax.ShapeDtypeStruct((nb, 1, MIX_W), F32),
                   jax.ShapeDtypeStruct((nb, 1, MIX_W), F32),
                   jax.ShapeDtypeStruct((nb, HEADS, HEAD_DIM, HEAD_DIM), F32),
                   jax.ShapeDtypeStruct((nb, 2, MIX_W), F32)],
        compiler_params=_params("parallel"),
        name="decode_mix",
    )(lg, proj_s, state_ret, state_conv, cos2, sin2, gn, conv_w,
      q_gain.reshape(DEPTH, 1, HEAD_DIM), k_gain.reshape(DEPTH, 1, HEAD_DIM))


PAGES_PER_STEP = 8
RING = 3
WINDOW = PAGES_PER_STEP * PAGE_SIZE
SCAN_TOKENS = 1024
LANES = 128
SUB_ROWS = 16


def _sb_decode_kernel(pt_ref, q_ref, bias_ref, k_hbm, v_hbm, o_ref, k_buf, v_buf, sem,
                      qb_scr, scan_scr, acc_scr, run_scr, *, n_seq, n_steps, n_pages, layer):
    s = pl.program_id(1)
    g = pl.program_id(0) * n_steps + s
    total = n_seq * n_steps
    head_row = lax.broadcasted_iota(jnp.int32, (SUB_ROWS, WINDOW), 0)

    def page_copies(step):
        slot = step % RING
        seq, win = step // n_steps, step % n_steps
        copies = []
        for p in range(PAGES_PER_STEP):
            page = pt_ref[seq, n_pages - (win + 1) * PAGES_PER_STEP + p]
            copies.append(pltpu.make_async_copy(k_hbm.at[layer, page], k_buf.at[slot, p], sem.at[0, slot]))
            copies.append(pltpu.make_async_copy(v_hbm.at[layer, page], v_buf.at[slot, p], sem.at[1, slot]))
        return copies

    @pl.when(g == 0)
    def _():
        for first in range(min(2, total)):
            for c in page_copies(first):
                c.start()

    @pl.when(g + 2 < total)
    def _():
        for c in page_copies(g + 2):
            c.start()

    for c in page_copies(g):
        c.wait()
    slot = g % RING

    def head_window(buf, h):
        return jnp.concatenate(
            [buf[slot, p, pl.ds(h, PAGE_SIZE, stride=HEADS), :].astype(BF16) for p in range(PAGES_PER_STEP)],
            axis=0)

    @pl.when(s == 0)
    def _():
        acc_scr[...] = jnp.zeros_like(acc_scr)
        run_scr[...] = jnp.zeros_like(run_scr)
        q = q_ref[...]
        for h in range(HEADS):
            qb_scr[:, h * HEAD_DIM:(h + 1) * HEAD_DIM] = jnp.where(head_row[:, :HEAD_DIM] == h, q, 0.0).astype(BF16)
        r2 = lax.broadcasted_iota(jnp.int32, (SCAN_TOKENS, SCAN_TOKENS + LANES), 0)
        c2 = lax.broadcasted_iota(jnp.int32, (SCAN_TOKENS, SCAN_TOKENS + LANES), 1)
        scan_scr[...] = jnp.where((r2 > c2) | (c2 >= SCAN_TOKENS), 1.0, 0.0).astype(BF16)

    k_all = jnp.concatenate([head_window(k_buf, h) for h in range(HEADS)], axis=1)
    z = _dot_nt(qb_scr[...], k_all) * (HEAD_DIM ** -0.5) + bias_ref[...]
    sp = _softplus_neg_abs(z)
    log_b = jnp.minimum(z, 0.0) - sp
    log_1mb = -jnp.maximum(z, 0.0) - sp
    hi, lo = _split_bf16(log_1mb)
    after = run_scr[...]
    suffix = []
    for seg in reversed(range(WINDOW // SCAN_TOKENS)):
        cols = slice(seg * SCAN_TOKENS, (seg + 1) * SCAN_TOKENS)
        r = _dot(jnp.concatenate([hi[:, cols], lo[:, cols]], axis=0), scan_scr[...])
        r = r[:SUB_ROWS, :] + r[SUB_ROWS:, :]
        suffix.insert(0, r[:, :SCAN_TOKENS] + jnp.concatenate([after] * (SCAN_TOKENS // LANES), axis=1))
        after = after + r[:, SCAN_TOKENS:]
    run_scr[...] = after
    a = jnp.exp(log_b + jnp.concatenate(suffix, axis=1))

    a_all = jnp.concatenate([jnp.where(head_row == h, a, 0.0).astype(BF16) for h in range(HEADS)], axis=1)
    v_all = jnp.concatenate([head_window(v_buf, h) for h in range(HEADS)], axis=0)
    acc = acc_scr[...] + _dot(a_all, v_all)
    acc_scr[...] = acc

    @pl.when(s == n_steps - 1)
    def _():
        o_ref[...] = acc[0:HEADS, :]


def sb_decode(q_rows, bias_col, cache_k, cache_v, page_table, layer):
    nb, n_pages = page_table.shape
    assert n_pages % PAGES_PER_STEP == 0, (n_pages, PAGES_PER_STEP)
    n_steps = n_pages // PAGES_PER_STEP

    page_buf = pltpu.VMEM((RING, PAGES_PER_STEP, PAGE_SIZE * HEADS, HEAD_DIM), F32)
    return pl.pallas_call(
        functools.partial(_sb_decode_kernel, n_seq=nb, n_steps=n_steps, n_pages=n_pages, layer=layer),
        grid_spec=pltpu.PrefetchScalarGridSpec(
            num_scalar_prefetch=1,
            grid=(nb, n_steps),
            in_specs=[pl.BlockSpec((None, SUB_ROWS, HEAD_DIM), lambda b, s, pt: (b, 0, 0)),
                      pl.BlockSpec((None, SUB_ROWS, 1), lambda b, s, pt: (layer, 0, 0)),
                      pl.BlockSpec(memory_space=pl.ANY), pl.BlockSpec(memory_space=pl.ANY)],
            out_specs=pl.BlockSpec((None, HEADS, HEAD_DIM), lambda b, s, pt: (b, 0, 0)),
            scratch_shapes=[page_buf, page_buf, pltpu.SemaphoreType.DMA((2, RING)),
                            pltpu.VMEM((SUB_ROWS, HEADS * HEAD_DIM), BF16),
                            pltpu.VMEM((SCAN_TOKENS, SCAN_TOKENS + LANES), BF16),
                            pltpu.VMEM((SUB_ROWS, HEAD_DIM), F32),
                            pltpu.VMEM((SUB_ROWS, LANES), F32)]),
        out_shape=jax.ShapeDtypeStruct((nb, HEADS, HEAD_DIM), F32),
        compiler_params=_params("arbitrary", "arbitrary"),
        name="sb_decode",
    )(page_table, q_rows, bias_col, cache_k, cache_v)


def _rope_tables(pos):
    half = HEAD_DIM // 2
    inv = ROPE_THETA ** (-jnp.arange(half, dtype=F32) / half)
    ang = pos.astype(F32)[:, None] * inv[None, :]
    cos, sin = jnp.cos(ang), jnp.sin(ang)
    return jnp.concatenate([cos, cos], axis=-1), jnp.concatenate([-sin, sin], axis=-1)


def kernel(x_prompt, x_sample, cache_sb_k, cache_sb_v, state_ret, state_conv, page_table, p_prompt, p_sample,
           ffn1_norm, ffn1_w_gu, ffn1_w_down, mix_norm, w_in, ret_gn, conv_w, sb_q_norm, sb_k_norm, sb_bias,
           w_branch_ret, w_branch_conv, w_branch_sb, w_out, ffn2_norm, ffn2_w_gu, ffn2_w_down,
           ple_norm, w_ple_gate, w_ple_up):
    batch, seq, _ = x_prompt.shape
    nb = x_sample.shape[0]
    m_p = batch * seq
    m_s = 16

    lg =jnp.log1p(-jnp.exp2(-5.0 - jnp.arange(HEADS, dtype=F32)))
    cos_p, sin_p = _rope_tables(jnp.arange(seq, dtype=jnp.int32))
    past_len = page_table.shape[1] * PAGE_SIZE
    cos_s, sin_s = _rope_tables(jnp.full((1,), past_len, jnp.int32))
    tp = lambda t: min(t, m_p)
    page_rows = cache_sb_k.shape[:2] + (PAGE_SIZE * HEADS, HEAD_DIM)
    cache_sb_k, cache_sb_v = cache_sb_k.reshape(page_rows), cache_sb_v.reshape(page_rows)
    row_pad = ((0, 0), (0, SUB_ROWS - HEADS), (0, 0))
    bias_cols = jnp.pad(sb_bias.reshape(DEPTH, HEADS, 1), row_pad)

    xp = x_prompt.reshape(m_p, D_MODEL)
    xs = jnp.pad(x_sample.reshape(nb, D_MODEL), ((0, m_s - nb), (0, 0)))
    pp = p_prompt.reshape(DEPTH, m_p, P_DIM)
    ps = jnp.pad(p_sample.reshape(DEPTH, nb, P_DIM), ((0, 0), (0, m_s - nb), (0, 0)))

    def ffn(stream, w_gu, w_down, next_gain, i):
        xp, xs, gp, gs, qp, qs = stream
        ap, a_s = dense("swiglu_mm", _swiglu_body, [(gp, None)], [(gs, None)],
                        [(w_gu, 0), (w_gu, D_FF)], i, D_FF, BF16, norm_in=(qp, qs))
        return dense("ffn_down", functools.partial(_residual_body, scale=0.5),
                     [(ap, None), (xp, 0)], [(a_s, None), (xs, 0)], [(w_down, 0)], i, D_MODEL, F32,
                     norm_out=(next_gain, i))

    stream = (xp, xs) + tuple(prenorm(xp, xs, ffn1_norm, 0, tp(512)))
    outs = [[] for _ in range(6)]
    kv_p = None
    for i in range(DEPTH):
        stream = ffn(stream, ffn1_w_gu, ffn1_w_down, mix_norm, i)
        xp, xs, gp, gs, qp, qs = stream
        proj, proj_s = dense("in_proj", _proj_body, [(gp, None)], [(gs, None)], [(w_in, 0)], i, N_IN, F32,
                             norm_in=(qp, qs))

        ya, s_p = retention_prompt(proj, cos_p, sin_p, lg, ret_gn, i, batch, seq)
        yb, c_p = conv_prompt(proj, conv_w, i, batch, seq)
        kv_p = new_kv(proj, sb_k_norm, i, kv_p, tp(512))
        yc = sb_prompt(proj, kv_p[0], sb_q_norm, sb_bias[i], i, batch, seq)

        ya_s, yb_s, sq_s, sk_s, s_s, c_s = decode_mix(
            proj_s[:nb].reshape(nb, 1, N_IN), state_ret, state_conv, cos_s, sin_s, lg, ret_gn, conv_w,
            sb_q_norm, sb_k_norm, i, nb)
        q_rows = jnp.pad(sq_s.reshape(nb, HEADS, HEAD_DIM), row_pad)
        yc_s = sb_decode(q_rows, bias_cols, cache_sb_k, cache_sb_v, page_table, i)
        yc_s = yc_s.reshape(nb, MIX_W).astype(BF16)
        pad_rows = lambda y: jnp.pad(y.reshape(nb, MIX_W), ((0, m_s - nb), (0, 0)))
        sv_s = proj_s[:nb, COL_SV * MIX_W:(COL_SV + 1) * MIX_W]

        def branch_rows(ya, yb, yc, proj):
            return [(ya, None), (yb, None), (yc, None),
                    (proj, COL_GA * MIX_W), (proj, COL_GB * MIX_W), (proj, COL_GC * MIX_W)]

        mp, ms = dense("merge_branches", _merge_body, branch_rows(ya, yb, yc, proj),
                       branch_rows(pad_rows(ya_s), pad_rows(yb_s), pad_rows(yc_s), proj_s),
                       [(w_branch_ret, 0), (w_branch_conv, 0), (w_branch_sb, 0)], i, D_MODEL, BF16)
        stream = dense("out_proj", functools.partial(_residual_body, scale=1.0),
                       [(mp, None), (xp, 0)], [(ms, None), (xs, 0)], [(w_out, 0)], i, D_MODEL, F32,
                       norm_out=(ffn2_norm, i))
        xp, xs, gp, gs, qp, qs = ffn(stream, ffn2_w_gu, ffn2_w_down, ple_norm, i)
        stream = dense("ple_update", _ple_body, [(gp, None), (pp[i], None), (xp, 0)],
                       [(gs, None), (ps[i], None), (xs, 0)], [(w_ple_gate, 0), (w_ple_up, 0)],
                       i, D_MODEL, F32, norm_in=(qp, qs),
                       norm_out=(ffn1_norm, i + 1) if i + 1 < DEPTH else None)
        xp, xs = stream[:2]

        for lst, val in zip(outs, (sk_s.reshape(nb, 1, HEADS, HEAD_DIM),
                                   sv_s.reshape(nb, 1, HEADS, HEAD_DIM),
                                   s_p, s_s, c_p, c_s)):
            lst.append(val)

    new_k_p, new_v_p = (a.reshape(DEPTH, batch, seq, HEADS, HEAD_DIM) for a in kv_p)
    return ((xp.reshape(batch, seq, D_MODEL), xs[:nb].reshape(nb, 1, D_MODEL), new_k_p, new_v_p)
            + tuple(jnp.stack(o) for o in outs))
```

```python
import functools

import jax
import jax.numpy as jnp
from jax import lax
from jax.experimental import pallas as pl
from jax.experimental.pallas import tpu as pltpu

F32 = jnp.float32
BF16 = jnp.bfloat16

D_MODEL = 2048
DEPTH = 2
PAGE_SIZE = 128
HEADS = 8
HEAD_DIM = 128
MIX_W = HEADS * HEAD_DIM
CHUNK = 128
D_FF = 5632
P_DIM = 256
ROPE_THETA = 10000.0
EPS = 1e-6
N_IN = 9 * MIX_W + 3 * D_MODEL
COL_RQ, COL_RK, COL_RV, COL_CH, COL_CB, COL_CC, COL_SQ, COL_SK, COL_SV = range(9)
COL_GA, COL_GB, COL_GC = 9, 11, 13

VMEM_LIMIT_BYTES = 56 * 1024 * 1024


def _params(*sem):
    return pltpu.CompilerParams(dimension_semantics=sem, vmem_limit_bytes=VMEM_LIMIT_BYTES)


def _dot(a, b):
    return jnp.dot(a, b, preferred_element_type=F32)


def _dot_nt(a, b):
    return lax.dot_general(a, b, (((1,), (1,)), ((), ())), preferred_element_type=F32)


def _dot_tn(a, b):
    return lax.dot_general(a, b, (((0,), (0,)), ((), ())), preferred_element_type=F32)


def _rms(x, g):
    return x * lax.rsqrt(jnp.mean(x * x, axis=-1, keepdims=True) + EPS) * g


def _sigmoid(x):
    return jax.nn.sigmoid(x)


def _row_scale(sumsq_parts):
    return lax.rsqrt(jnp.sum(sumsq_parts, axis=0) * (1.0 / D_MODEL) + EPS)


def _scaled(acc, r):
    return acc * jnp.concatenate([r] * (acc.shape[1] // LANES), axis=1)


def _gain_and_sumsq(x, gain, gained_ref, sumsq_ref):
    gained_ref[...] = (x * gain).astype(gained_ref.dtype)
    sumsq_ref[...] = jnp.broadcast_to(jnp.sum(x * x, axis=1, keepdims=True), sumsq_ref.shape)


def _prenorm_kernel(xp_ref, xs_ref, g_ref, gp_ref, gs_ref, qp_ref, qs_ref):
    _gain_and_sumsq(xp_ref[...], g_ref[...], gp_ref, qp_ref)

    @pl.when(pl.program_id(0) == 0)
    def _():
        _gain_and_sumsq(xs_ref[...], g_ref[...], gs_ref, qs_ref)


def prenorm(xp, xs, g, layer, tm):
    m, d = xp.shape
    ms = xs.shape[0]
    return pl.pallas_call(
        _prenorm_kernel,
        grid=(m // tm,),
        in_specs=[pl.BlockSpec((tm, d), lambda i: (i, 0)),
                  pl.BlockSpec((ms, d), lambda i: (0, 0)),
                  pl.BlockSpec((None, 1, d), lambda i: (layer, 0, 0))],
        out_specs=[pl.BlockSpec((tm, d), lambda i: (i, 0)),
                   pl.BlockSpec((ms, d), lambda i: (0, 0)),
                   pl.BlockSpec((None, tm, LANES), lambda i: (0, i, 0)),
                   pl.BlockSpec((None, ms, LANES), lambda i: (0, 0, 0))],
        out_shape=[jax.ShapeDtypeStruct((m, d), BF16), jax.ShapeDtypeStruct((ms, d), BF16),
                   jax.ShapeDtypeStruct((1, m, LANES), F32), jax.ShapeDtypeStruct((1, ms, LANES), F32)],
        compiler_params=_params("arbitrary"),
        name="prenorm",
    )(xp, xs, g.reshape(DEPTH, 1, d))


def _dense_kernel(*refs, body, n_rows, n_w, norm_in, norm_out):
    refs = iter(refs)
    take = lambda n: [next(refs) for _ in range(n)]
    rows_p, rows_s = take(n_rows), take(n_rows)
    sumsq_in = take(2) if norm_in else None
    first_refs, next_refs = take(n_w), take(n_w)
    gain_ref = take(1)[0] if norm_out else None
    op_ref, os_ref = take(2)
    norm_refs = take(4) if norm_out else None
    w_scr = take(n_w)
    scale_scr = take(2) if norm_in else None
    j, i = pl.program_id(0), pl.program_id(1)
    cur = j % 2

    @pl.when((j == 0) & (i == 0))
    def _():
        for first, scr in zip(first_refs, w_scr):
            scr[0] = first[...].astype(BF16)
        if norm_in:
            scale_scr[1][...] = _row_scale(sumsq_in[1][...])

    for nxt, scr in zip(next_refs, w_scr):
        rows = nxt.shape[0]
        scr[1 - cur, pl.ds(pl.multiple_of(i * rows, rows), rows), :] = nxt[...].astype(BF16)

    if norm_in:
        @pl.when(j == 0)
        def _():
            scale_scr[0][i] = _row_scale(sumsq_in[0][...])

    def emit(rows, scale, o_ref, gained_ref, sumsq_ref):
        out = body([r[...] for r in rows], [scr[cur] for scr in w_scr], scale)
        o_ref[...] = out.astype(o_ref.dtype)
        if norm_out:
            _gain_and_sumsq(out, gain_ref[...], gained_ref, sumsq_ref)

    @pl.when(i == 0)
    def _():
        emit(rows_s, scale_scr[1][...] if norm_in else None, os_ref,
             norm_refs[1] if norm_out else None, norm_refs[3] if norm_out else None)

    emit(rows_p, scale_scr[0][i] if norm_in else None, op_ref,
         norm_refs[0] if norm_out else None, norm_refs[2] if norm_out else None)


DENSE_TILES = {
    "swiglu_mm": (1024, 512), "ffn_down": (512, 512), "in_proj": (1024, 1024),
    "merge_branches": (512, 1024), "out_proj": (512, 1024), "ple_update": (512, 1024),
}


def dense(name, body, rows_p, rows_s, weights, layer, n_out, out_dtype, norm_in=None, norm_out=None):
    tm, tn = DENSE_TILES[name]
    rows_p = [(a, c if c is None else c // tn) for a, c in rows_p]
    rows_s = [(a, c if c is None else c // tn) for a, c in rows_s]
    weights = [(a, c // tn) for a, c in weights]
    m_p, m_s = rows_p[0][0].shape[0], rows_s[0][0].shape[0]
    tm = min(tm, m_p)

    def row_specs(rows, t, tiled):
        specs = []
        for arr, col in rows:
            if col is None:
                specs.append(pl.BlockSpec((t, arr.shape[1]), (lambda j, i: (i, 0)) if tiled else (lambda j, i: (0, 0))))
            elif tiled:
                specs.append(pl.BlockSpec((t, tn), lambda j, i, col=col: (i, col + j)))
            else:
                specs.append(pl.BlockSpec((t, tn), lambda j, i, col=col: (0, col + j)))
        return specs

    n_j, n_i = n_out // tn, m_p // tm
    first_specs = [pl.BlockSpec((None, w.shape[1], tn), lambda j, i, col=col: (layer, 0, col),
                                pipeline_mode=pl.Buffered(1)) for w, col in weights]
    for w, _ in weights:
        assert w.shape[1] % (n_i * SUB_ROWS) == 0, (name, w.shape, n_i)
    next_specs = [pl.BlockSpec((None, w.shape[1] // n_i, tn),
                               lambda j, i, col=col: (layer, i, col + jnp.minimum(j + 1, n_j - 1)))
                  for w, col in weights]
    tile_p = pl.BlockSpec((tm, tn), lambda j, i: (i, j))
    tile_s = pl.BlockSpec((m_s, tn), lambda j, i: (0, j))
    in_specs = row_specs(rows_p, tm, True) + row_specs(rows_s, m_s, False)
    operands = [a for a, _ in rows_p] + [a for a, _ in rows_s]
    out_specs, out_shape = [tile_p, tile_s], [jax.ShapeDtypeStruct((m_p, n_out), out_dtype),
                                              jax.ShapeDtypeStruct((m_s, n_out), out_dtype)]
    scratch = [pltpu.VMEM((2, w.shape[1], tn), BF16) for w, _ in weights]
    if norm_in is not None:
        parts = norm_in[0].shape[0]
        in_specs += [pl.BlockSpec((parts, tm, LANES), lambda j, i: (0, jnp.where(j == 0, i, n_i - 1), 0)),
                     pl.BlockSpec((parts, m_s, LANES), lambda j, i: (0, 0, 0))]
        operands += list(norm_in)
        scratch += [pltpu.VMEM((n_i, tm, LANES), F32), pltpu.VMEM((m_s, LANES), F32)]
    in_specs += first_specs + next_specs
    operands += [w for w, _ in weights] * 2
    if norm_out is not None:
        gain, gain_layer = norm_out
        in_specs.append(pl.BlockSpec((None, 1, tn), lambda j, i: (gain_layer, 0, j)))
        operands.append(gain.reshape(DEPTH, 1, n_out))
        out_specs += [tile_p, tile_s, pl.BlockSpec((None, tm, LANES), lambda j, i: (j, i, 0)),
                      pl.BlockSpec((None, m_s, LANES), lambda j, i: (j, 0, 0))]
        out_shape += [jax.ShapeDtypeStruct((m_p, n_out), BF16), jax.ShapeDtypeStruct((m_s, n_out), BF16),
                      jax.ShapeDtypeStruct((n_j, m_p, LANES), F32), jax.ShapeDtypeStruct((n_j, m_s, LANES), F32)]
    return pl.pallas_call(
        functools.partial(_dense_kernel, body=body, n_rows=len(rows_p), n_w=len(weights),
                          norm_in=norm_in is not None, norm_out=norm_out is not None),
        grid=(n_j, n_i),
        in_specs=in_specs,
        out_specs=out_specs,
        out_shape=out_shape,
        scratch_shapes=scratch,
        compiler_params=_params("arbitrary", "arbitrary"),
        name=name,
    )(*operands)


def _swiglu_body(rows, ws, r):
    (h,), (wg, wu) = rows, ws
    g = _scaled(_dot(h, wg), r)
    return g * _sigmoid(g) * _scaled(_dot(h, wu), r)


def _residual_body(rows, ws, r, *, scale):
    (a, x), (w,) = rows, ws
    return x + scale * _dot(a, w)


def _proj_body(rows, ws, r):
    return _scaled(_dot(rows[0], ws[0]), r)


def _merge_body(rows, ws, r):
    (ya, yb, yc, ga, gb, gc), (wr, wc, wsb) = rows, ws
    return _sigmoid(ga) * _dot(ya, wr) + _sigmoid(gb) * _dot(yb, wc) + _sigmoid(gc) * _dot(yc, wsb)


def _ple_body(rows, ws, r):
    (h, p, x), (wg, wu) = rows, ws
    return x + _sigmoid(_scaled(_dot(h, wg), r)) * _dot(p.astype(BF16), wu)


def _rope(x, cos2, sin2):
    return x * cos2 + pltpu.roll(x, HEAD_DIM // 2, 1) * sin2


def _softplus_neg_abs(z):
    neg_abs = lax.bitcast_convert_type(
        lax.bitcast_convert_type(z, jnp.uint32) | jnp.uint32(0x80000000), F32)
    return jnp.log(1.0 + jnp.exp(neg_abs))


def _split_bf16(x):
    hi = x.astype(BF16)
    return hi, (x - hi.astype(F32)).astype(BF16)


def _retention_kernel(lg_ref, q_ref, k_ref, v_ref, cos_ref, sin_ref, gn_ref, ya_ref, s_out_ref, s_scr,
                      *, n_chunks):
    c = pl.program_id(1)

    @pl.when(c == 0)
    def _():
        s_scr[...] = jnp.zeros_like(s_scr)

    cos2 = cos_ref[...]
    sin2 = sin_ref[...]
    ii = lax.broadcasted_iota(jnp.int32, (CHUNK, CHUNK), 0)
    jj = lax.broadcasted_iota(jnp.int32, (CHUNK, CHUNK), 1)
    diff = (ii - jj).astype(F32)
    pos = lax.broadcasted_iota(jnp.int32, (CHUNK, 1), 0).astype(F32)
    heads = range(HEADS)
    cols = [slice(h * HEAD_DIM, (h + 1) * HEAD_DIM) for h in heads]
    lg = [lg_ref[h] for h in heads]
    qb, kb, kd, vb = [], [], [], []
    for h in heads:
        k = _rope(k_ref[:, cols[h]], cos2, sin2) * (HEAD_DIM ** -0.5)
        qb.append(_rope(q_ref[:, cols[h]], cos2, sin2).astype(BF16))
        kb.append(k.astype(BF16))
        kd.append((k * jnp.exp(lg[h] * (CHUNK - 1.0 - pos))).astype(BF16))
        vb.append(v_ref[:, cols[h]].astype(BF16))
    state = [s_scr[h] for h in heads]
    scores = [_dot_nt(qb[h], kb[h]) for h in heads]
    cross = [_dot(qb[h], state[h].astype(BF16)) for h in heads]
    update = [_dot_tn(kd[h], vb[h]) for h in heads]
    inner = []
    for h in heads:
        dmask = jnp.where(diff >= 0, jnp.exp(lg[h] * jnp.maximum(diff, 0.0)), 0.0)
        inner.append(_dot((scores[h] * dmask).astype(BF16), vb[h]))
    for h in heads:
        o = inner[h] + cross[h] * jnp.exp(lg[h] * (pos + 1.0))
        ya_ref[:, cols[h]] = _rms(o, gn_ref[h:h + 1, :]).astype(ya_ref.dtype)
        s_scr[h] = jnp.exp(lg[h] * CHUNK) * state[h] + update[h]

    @pl.when(c == n_chunks - 1)
    def _():
        s_out_ref[...] = s_scr[...]


def retention_prompt(proj, cos2, sin2, lg, gn, layer, batch, seq):
    n_chunks = seq // CHUNK
    m = batch * seq

    def col_spec(col):
        return pl.BlockSpec((CHUNK, MIX_W), lambda b, c, lg_ref: (b * n_chunks + c, col))

    tab_spec = pl.BlockSpec((CHUNK, HEAD_DIM), lambda b, c, lg_ref: (c, 0))
    return pl.pallas_call(
        functools.partial(_retention_kernel, n_chunks=n_chunks),
        grid_spec=pltpu.PrefetchScalarGridSpec(
            num_scalar_prefetch=1,
            grid=(batch, n_chunks),
            in_specs=[col_spec(COL_RQ), col_spec(COL_RK), col_spec(COL_RV), tab_spec, tab_spec,
                      pl.BlockSpec((None, HEADS, HEAD_DIM), lambda b, c, lg_ref: (layer, 0, 0))],
            out_specs=[pl.BlockSpec((CHUNK, MIX_W), lambda b, c, lg_ref: (b * n_chunks + c, 0)),
                       pl.BlockSpec((None, HEADS, HEAD_DIM, HEAD_DIM), lambda b, c, lg_ref: (b, 0, 0, 0))],
            scratch_shapes=[pltpu.VMEM((HEADS, HEAD_DIM, HEAD_DIM), F32)]),
        out_shape=[jax.ShapeDtypeStruct((m, MIX_W), BF16),
                   jax.ShapeDtypeStruct((batch, HEADS, HEAD_DIM, HEAD_DIM), F32)],
        compiler_params=_params("parallel", "arbitrary"),
        name="retention_prompt",
    )(lg, proj, proj, proj, cos2, sin2, gn)


def _conv_kernel(ch_ref, cb_ref, cc_ref, w_ref, yb_ref, cn_ref):
    u = cc_ref[...] * ch_ref[...]
    seq = u.shape[0]
    t = lax.broadcasted_iota(jnp.int32, u.shape, 0)
    u1 = jnp.where(t >= 1, pltpu.roll(u, 1, 0), 0.0)
    u2 = jnp.where(t >= 2, pltpu.roll(u, 2, 0), 0.0)
    cv = u2 * w_ref[0:1, :] + u1 * w_ref[1:2, :] + u * w_ref[2:3, :]
    yb_ref[...] = (cb_ref[...] * cv).astype(yb_ref.dtype)
    cn_ref[...] = cc_ref[seq - 2:seq, :] * ch_ref[seq - 2:seq, :]


def conv_prompt(proj, conv_w, layer, batch, seq):
    tw = 256
    nw = MIX_W // tw

    def col_spec(col):
        return pl.BlockSpec((seq, tw), lambda b, j: (b, col * nw + j))

    return pl.pallas_call(
        _conv_kernel,
        grid=(batch, nw),
        in_specs=[col_spec(COL_CH), col_spec(COL_CB), col_spec(COL_CC),
                  pl.BlockSpec((None, 3, tw), lambda b, j: (layer, 0, j))],
        out_specs=[pl.BlockSpec((seq, tw), lambda b, j: (b, j)),
                   pl.BlockSpec((None, 2, tw), lambda b, j: (b, 0, j))],
        out_shape=[jax.ShapeDtypeStruct((batch * seq, MIX_W), BF16),
                   jax.ShapeDtypeStruct((batch, 2, MIX_W), F32)],
        compiler_params=_params("parallel", "parallel"),
        name="conv_prompt",
    )(proj, proj, proj, conv_w)


def _new_kv_kernel(k_ref, v_ref, g_ref, *refs, layer, first):
    ok_ref, ov_ref = refs[-2:]

    def write_layer():
        g = g_ref[...]
        for h in range(HEADS):
            cols = slice(h * HEAD_DIM, (h + 1) * HEAD_DIM)
            ok_ref[:, cols] = _rms(k_ref[:, cols], g)
        ov_ref[...] = v_ref[...]

    if first:
        pl.when(pl.program_id(0) == layer)(write_layer)

        @pl.when(pl.program_id(0) != layer)
        def _():
            ok_ref[...] = jnp.zeros_like(ok_ref)
            ov_ref[...] = jnp.zeros_like(ov_ref)
    else:
        write_layer()


def new_kv(proj, g, layer, prev, tm):
    m = proj.shape[0]
    n_i = m // tm
    first = prev is None
    if first:
        grid = (DEPTH, n_i)
        row = lambda l, i: jnp.where(l == layer, i, 0)
        in_specs = [pl.BlockSpec((tm, MIX_W), lambda l, i: (row(l, i), COL_SK)),
                    pl.BlockSpec((tm, MIX_W), lambda l, i: (row(l, i), COL_SV)),
                    pl.BlockSpec((None, 1, HEAD_DIM), lambda l, i: (layer, 0, 0))]
        out_spec = pl.BlockSpec((None, tm, MIX_W), lambda l, i: (l, i, 0))
        operands, aliases, sem = [proj, proj, g.reshape(DEPTH, 1, HEAD_DIM)], {}, ("arbitrary", "arbitrary")
    else:
        grid = (n_i,)
        in_specs = [pl.BlockSpec((tm, MIX_W), lambda i: (i, COL_SK)),
                    pl.BlockSpec((tm, MIX_W), lambda i: (i, COL_SV)),
                    pl.BlockSpec((None, 1, HEAD_DIM), lambda i: (layer, 0, 0)),
                    pl.BlockSpec(memory_space=pl.ANY), pl.BlockSpec(memory_space=pl.ANY)]
        out_spec = pl.BlockSpec((None, tm, MIX_W), lambda i: (layer, i, 0))
        operands = [proj, proj, g.reshape(DEPTH, 1, HEAD_DIM), prev[0], prev[1]]
        aliases, sem = {3: 0, 4: 1}, ("arbitrary",)
    return pl.pallas_call(
        functools.partial(_new_kv_kernel, layer=layer, first=first),
        grid=grid,
        in_specs=in_specs,
        out_specs=[out_spec, out_spec],
        out_shape=[jax.ShapeDtypeStruct((DEPTH, m, MIX_W), F32)] * 2,
        input_output_aliases=aliases,
        compiler_params=_params(*sem),
        name="new_kv",
    )(*operands)


SB_TILE = 256


SB_HEADS_PER_STEP = 8


def _sb_prompt_kernel(bias_ref, q_ref, k_ref, v_ref, qg_ref, o_ref, acc_scr, run_scr):
    t = SB_TILE
    w = HEAD_DIM
    hg = pl.program_id(1)
    qi = pl.program_id(2)
    row = lax.broadcasted_iota(jnp.int32, (t, t), 0)
    col = lax.broadcasted_iota(jnp.int32, (t, t), 1)
    causal = col < row
    key = jnp.bitwise_and(lax.broadcasted_iota(jnp.int32, (2 * w, 2 * w), 0), w - 1)
    c2 = lax.broadcasted_iota(jnp.int32, (2 * w, 2 * w), 1)
    scan = jnp.where((key > c2) | (c2 >= w), 1.0, 0.0).astype(BF16)
    acc_scr[...] = jnp.zeros_like(acc_scr)
    run_scr[...] = jnp.zeros_like(run_scr)
    heads = range(SB_HEADS_PER_STEP)
    cols = [slice(hh * HEAD_DIM, (hh + 1) * HEAD_DIM) for hh in heads]
    qn = [_rms(q_ref[:, cols[hh]], qg_ref[...]).astype(BF16) for hh in heads]
    bias = [bias_ref[hg * SB_HEADS_PER_STEP + hh] for hh in heads]

    def block(kj, diagonal):
        start = pl.multiple_of(kj * t, t)
        scores = [_dot_nt(qn[hh], k_ref[pl.ds(start, t), cols[hh]].astype(BF16)) for hh in heads]
        zs, us, scanned = [], [], []
        for hh in heads:
            z = scores[hh] * (HEAD_DIM ** -0.5) + bias[hh]
            u = jnp.maximum(z, 0.0) + _softplus_neg_abs(z)
            if diagonal:
                u = jnp.where(causal, u, 0.0)
            halves = []
            for part in (u[:, :w], u[:, w:]):
                hi, lo = _split_bf16(part)
                halves.append(_dot(jnp.concatenate([hi, lo], axis=1), scan))
            zs.append(z)
            us.append(u)
            scanned.append(halves)
        weights = []
        for hh in heads:
            (left, right), run = scanned[hh], run_scr[hh]
            right_of_left = right[:, w:] + run
            s_left = us[hh][:, :w] + left[:, :w] + right_of_left
            s_right = us[hh][:, w:] + right[:, :w] + run
            e = jnp.exp(zs[hh] - jnp.concatenate([s_left, s_right], axis=1))
            if diagonal:
                e = jnp.where(causal, e, 0.0)
            weights.append(e.astype(BF16))
            run_scr[hh] = left[:, w:] + right_of_left
        for hh in heads:
            acc_scr[hh] += _dot(weights[hh], v_ref[pl.ds(start, t), cols[hh]].astype(BF16))

    block(qi, True)

    def body(i, carry):
        block(qi - 1 - i, False)
        return carry

    lax.fori_loop(0, qi, body, 0)
    for hh in heads:
        o_ref[:, cols[hh]] = acc_scr[hh].astype(o_ref.dtype)


def sb_prompt(proj, sk, q_gain, bias, layer, batch, seq):
    t = SB_TILE
    nq = seq // t
    hs = SB_HEADS_PER_STEP
    width = hs * HEAD_DIM
    groups = HEADS // hs
    return pl.pallas_call(
        _sb_prompt_kernel,
        grid_spec=pltpu.PrefetchScalarGridSpec(
            num_scalar_prefetch=1,
            grid=(batch, groups, nq),
            in_specs=[pl.BlockSpec((t, width), lambda b, g, qi, s: (b * nq + qi, COL_SQ * groups + g)),
                      pl.BlockSpec((None, seq, width), lambda b, g, qi, s: (layer, b, g)),
                      pl.BlockSpec((seq, width), lambda b, g, qi, s: (b, COL_SV * groups + g)),
                      pl.BlockSpec((None, 1, HEAD_DIM), lambda b, g, qi, s: (layer, 0, 0))],
            out_specs=pl.BlockSpec((t, width), lambda b, g, qi, s: (b * nq + qi, g)),
            scratch_shapes=[pltpu.VMEM((hs, t, HEAD_DIM), F32), pltpu.VMEM((hs, t, HEAD_DIM), F32)]),
        out_shape=jax.ShapeDtypeStruct((batch * seq, MIX_W), BF16),
        compiler_params=_params("parallel", "parallel", "parallel"),
        name="sb_prompt",
    )(bias, proj, sk, proj, q_gain.reshape(DEPTH, 1, HEAD_DIM))


def _decode_mix_kernel(lg_ref, p_ref, s_ref, cprev_ref, cos_ref, sin_ref, gn_ref, cw_ref, qg_ref, kg_ref,
                       ya_ref, yb_ref, sq_ref, sk_ref, s_out_ref, cn_ref):
    cos2 = cos_ref[...]
    sin2 = sin_ref[...]
    row = lax.broadcasted_iota(jnp.int32, (HEAD_DIM, HEAD_DIM), 0)
    col = lax.broadcasted_iota(jnp.int32, (HEAD_DIM, HEAD_DIM), 1)

    def stream(c, h):
        return p_ref[:, c * MIX_W + h * HEAD_DIM:c * MIX_W + (h + 1) * HEAD_DIM]

    def bf_round(x):
        return x.astype(BF16).astype(F32)

    for h in range(HEADS):
        cols = slice(h * HEAD_DIM, (h + 1) * HEAD_DIM)
        lg = lg_ref[h]
        q = bf_round(_rope(stream(COL_RQ, h), cos2, sin2))
        k = bf_round(_rope(stream(COL_RK, h), cos2, sin2) * (HEAD_DIM ** -0.5))
        v = bf_round(stream(COL_RV, h))
        s = s_ref[h]
        score = bf_round(jnp.sum(q * k, axis=-1, keepdims=True))
        inner = score * v
        q_rows = jnp.broadcast_to(q, (16, HEAD_DIM)).astype(BF16)
        cross = _dot(q_rows, s.astype(BF16))[0:1, :] * jnp.exp(lg * jnp.ones((1, 1), F32))
        k_col = jnp.sum(jnp.where(row == col, jnp.broadcast_to(k, (HEAD_DIM, HEAD_DIM)), 0.0),
                        axis=1, keepdims=True)
        s_out_ref[h] = jnp.exp(lg * jnp.ones((1, 1), F32)) * s + k_col * v
        ya_ref[:, cols] = _rms(inner + cross, gn_ref[h:h + 1, :]).astype(ya_ref.dtype)
        sq_ref[:, cols] = _rms(stream(COL_SQ, h), qg_ref[...])
        sk_ref[:, cols] = _rms(stream(COL_SK, h), kg_ref[...])

    ch = p_ref[:, COL_CH * MIX_W:(COL_CH + 1) * MIX_W]
    cb = p_ref[:, COL_CB * MIX_W:(COL_CB + 1) * MIX_W]
    cc = p_ref[:, COL_CC * MIX_W:(COL_CC + 1) * MIX_W]
    u = cc * ch
    cv = cprev_ref[0:1, :] * cw_ref[0:1, :] + cprev_ref[1:2, :] * cw_ref[1:2, :] + u * cw_ref[2:3, :]
    yb_ref[...] = (cb * cv).astype(yb_ref.dtype)
    cn_ref[0:1, :] = cprev_ref[1:2, :]
    cn_ref[1:2, :] = u


def decode_mix(proj_s, state_ret, state_conv, cos2, sin2, lg, gn, conv_w, q_gain, k_gain, layer, nb):
    def row_spec(width):
        return pl.BlockSpec((None, 1, width), lambda b, s: (b, 0, 0))

    def w_spec(*shape):
        return pl.BlockSpec((None,) + shape, lambda b, s: (layer,) + (0,) * len(shape))

    tab_spec = pl.BlockSpec((1, HEAD_DIM), lambda b, s: (0, 0))
    return pl.pallas_call(
        _decode_mix_kernel,
        grid_spec=pltpu.PrefetchScalarGridSpec(
            num_scalar_prefetch=1,
            grid=(nb,),
            in_specs=[row_spec(N_IN),
                      pl.BlockSpec((None, None, HEADS, HEAD_DIM, HEAD_DIM), lambda b, s: (layer, b, 0, 0, 0)),
                      pl.BlockSpec((None, None, 2, MIX_W), lambda b, s: (layer, b, 0, 0)),
                      tab_spec, tab_spec, w_spec(HEADS, HEAD_DIM), w_spec(3, MIX_W),
                      w_spec(1, HEAD_DIM), w_spec(1, HEAD_DIM)],
            out_specs=[row_spec(MIX_W), row_spec(MIX_W), row_spec(MIX_W), row_spec(MIX_W),
                       pl.BlockSpec((None, HEADS, HEAD_DIM, HEAD_DIM), lambda b, s: (b, 0, 0, 0)),
                       pl.BlockSpec((None, 2, MIX_W), lambda b, s: (b, 0, 0))]),
        out_shape=[jax.ShapeDtypeStruct((nb, 1, MIX_W), BF16),
                   jax.ShapeDtypeStruct((nb, 1, MIX_W), BF16),
                   jax.ShapeDtypeStruct((nb, 1, MIX_W), F32),
                   jax.ShapeDtypeStruct((nb, 1, MIX_W), F32),
                   jax.ShapeDtypeStruct((nb, HEADS, HEAD_DIM, HEAD_DIM), F32),
                   jax.ShapeDtypeStruct((nb, 2, MIX_W), F32)],
        compiler_params=_params("parallel"),
        name="decode_mix",
    )(lg, proj_s, state_ret, state_conv, cos2, sin2, gn, conv_w,
      q_gain.reshape(DEPTH, 1, HEAD_DIM), k_gain.reshape(DEPTH, 1, HEAD_DIM))


PAGES_PER_STEP = 8
RING = 3
WINDOW = PAGES_PER_STEP * PAGE_SIZE
SCAN_TOKENS = 1024
LANES = 128
SUB_ROWS = 16


def _sb_decode_kernel(pt_ref, q_ref, bias_ref, k_hbm, v_hbm, o_ref, k_buf, v_buf, sem,
                      qb_scr, scan_scr, acc_scr, run_scr, *, n_seq, n_steps, n_pages, layer):
    s = pl.program_id(1)
    g = pl.program_id(0) * n_steps + s
    total = n_seq * n_steps
    head_row = lax.broadcasted_iota(jnp.int32, (SUB_ROWS, WINDOW), 0)

    def page_copies(step):
        slot = step % RING
        seq, win = step // n_steps, step % n_steps
        copies = []
        for p in range(PAGES_PER_STEP):
            page = pt_ref[seq, n_pages - (win + 1) * PAGES_PER_STEP + p]
            copies.append(pltpu.make_async_copy(k_hbm.at[layer, page], k_buf.at[slot, p], sem.at[0, slot]))
            copies.append(pltpu.make_async_copy(v_hbm.at[layer, page], v_buf.at[slot, p], sem.at[1, slot]))
        return copies

    def start_step(step):
        for n, c in enumerate(page_copies(step)):
            c.start(priority=n % 2)

    @pl.when(g == 0)
    def _():
        for first in range(min(2, total)):
            start_step(first)

    @pl.when(g + 2 < total)
    def _():
        start_step(g + 2)

    for c in page_copies(g):
        c.wait()
    slot = g % RING

    def head_window(buf, h):
        return jnp.concatenate(
            [buf[slot, p, pl.ds(h, PAGE_SIZE, stride=HEADS), :].astype(BF16) for p in range(PAGES_PER_STEP)],
            axis=0)

    @pl.when(s == 0)
    def _():
        acc_scr[...] = jnp.zeros_like(acc_scr)
        run_scr[...] = jnp.zeros_like(run_scr)
        q = q_ref[...]
        for h in range(HEADS):
            qb_scr[:, h * HEAD_DIM:(h + 1) * HEAD_DIM] = jnp.where(head_row[:, :HEAD_DIM] == h, q, 0.0).astype(BF16)
        r2 = lax.broadcasted_iota(jnp.int32, (SCAN_TOKENS, SCAN_TOKENS + LANES), 0)
        c2 = lax.broadcasted_iota(jnp.int32, (SCAN_TOKENS, SCAN_TOKENS + LANES), 1)
        scan_scr[...] = jnp.where((r2 > c2) | (c2 >= SCAN_TOKENS), 1.0, 0.0).astype(BF16)

    k_all = jnp.concatenate([head_window(k_buf, h) for h in range(HEADS)], axis=1)
    z = _dot_nt(qb_scr[...], k_all) * (HEAD_DIM ** -0.5) + bias_ref[...]
    sp = _softplus_neg_abs(z)
    log_b = jnp.minimum(z, 0.0) - sp
    log_1mb = -jnp.maximum(z, 0.0) - sp
    hi, lo = _split_bf16(log_1mb)
    after = run_scr[...]
    suffix = []
    for seg in reversed(range(WINDOW // SCAN_TOKENS)):
        cols = slice(seg * SCAN_TOKENS, (seg + 1) * SCAN_TOKENS)
        r = _dot(jnp.concatenate([hi[:, cols], lo[:, cols]], axis=0), scan_scr[...])
        r = r[:SUB_ROWS, :] + r[SUB_ROWS:, :]
        suffix.insert(0, r[:, :SCAN_TOKENS] + jnp.concatenate([after] * (SCAN_TOKENS // LANES), axis=1))
        after = after + r[:, SCAN_TOKENS:]
    run_scr[...] = after
    a = jnp.exp(log_b + jnp.concatenate(suffix, axis=1))

    a_all = jnp.concatenate([jnp.where(head_row == h, a, 0.0).astype(BF16) for h in range(HEADS)], axis=1)
    v_all = jnp.concatenate([head_window(v_buf, h) for h in range(HEADS)], axis=0)
    acc = acc_scr[...] + _dot(a_all, v_all)
    acc_scr[...] = acc

    @pl.when(s == n_steps - 1)
    def _():
        o_ref[...] = acc[0:HEADS, :]


def sb_decode(q_rows, bias_col, cache_k, cache_v, page_table, layer):
    nb, n_pages = page_table.shape
    assert n_pages % PAGES_PER_STEP == 0, (n_pages, PAGES_PER_STEP)
    n_steps = n_pages // PAGES_PER_STEP

    page_buf = pltpu.VMEM((RING, PAGES_PER_STEP, PAGE_SIZE * HEADS, HEAD_DIM), F32)
    return pl.pallas_call(
        functools.partial(_sb_decode_kernel, n_seq=nb, n_steps=n_steps, n_pages=n_pages, layer=layer),
        grid_spec=pltpu.PrefetchScalarGridSpec(
            num_scalar_prefetch=1,
            grid=(nb, n_steps),
            in_specs=[pl.BlockSpec((None, SUB_ROWS, HEAD_DIM), lambda b, s, pt: (b, 0, 0)),
                      pl.BlockSpec((None, SUB_ROWS, 1), lambda b, s, pt: (layer, 0, 0)),
                      pl.BlockSpec(memory_space=pl.ANY), pl.BlockSpec(memory_space=pl.ANY)],
            out_specs=pl.BlockSpec((None, HEADS, HEAD_DIM), lambda b, s, pt: (b, 0, 0)),
            scratch_shapes=[page_buf, page_buf, pltpu.SemaphoreType.DMA((2, RING)),
                            pltpu.VMEM((SUB_ROWS, HEADS * HEAD_DIM), BF16),
                            pltpu.VMEM((SCAN_TOKENS, SCAN_TOKENS + LANES), BF16),
                            pltpu.VMEM((SUB_ROWS, HEAD_DIM), F32),
                            pltpu.VMEM((SUB_ROWS, LANES), F32)]),
        out_shape=jax.ShapeDtypeStruct((nb, HEADS, HEAD_DIM), F32),
        compiler_params=_params("arbitrary", "arbitrary"),
        name="sb_decode",
    )(page_table, q_rows, bias_col, cache_k, cache_v)


def _rope_tables(pos):
    half = HEAD_DIM // 2
    inv = ROPE_THETA ** (-jnp.arange(half, dtype=F32) / half)
    ang = pos.astype(F32)[:, None] * inv[None, :]
    cos, sin = jnp.cos(ang), jnp.sin(ang)
    return jnp.concatenate([cos, cos], axis=-1), jnp.concatenate([-sin, sin], axis=-1)


def kernel(x_prompt, x_sample, cache_sb_k, cache_sb_v, state_ret, state_conv, page_table, p_prompt, p_sample,
           ffn1_norm, ffn1_w_gu, ffn1_w_down, mix_norm, w_in, ret_gn, conv_w, sb_q_norm, sb_k_norm, sb_bias,
           w_branch_ret, w_branch_conv, w_branch_sb, w_out, ffn2_norm, ffn2_w_gu, ffn2_w_down,
           ple_norm, w_ple_gate, w_ple_up):
    batch, seq, _ = x_prompt.shape
    nb = x_sample.shape[0]
    m_p = batch * seq
    m_s = 16

    lg =jnp.log1p(-jnp.exp2(-5.0 - jnp.arange(HEADS, dtype=F32)))
    cos_p, sin_p = _rope_tables(jnp.arange(seq, dtype=jnp.int32))
    past_len = page_table.shape[1] * PAGE_SIZE
    cos_s, sin_s = _rope_tables(jnp.full((1,), past_len, jnp.int32))
    tp = lambda t: min(t, m_p)
    page_rows = cache_sb_k.shape[:2] + (PAGE_SIZE * HEADS, HEAD_DIM)
    cache_sb_k, cache_sb_v = cache_sb_k.reshape(page_rows), cache_sb_v.reshape(page_rows)
    row_pad = ((0, 0), (0, SUB_ROWS - HEADS), (0, 0))
    bias_cols = jnp.pad(sb_bias.reshape(DEPTH, HEADS, 1), row_pad)

    xp = x_prompt.reshape(m_p, D_MODEL)
    xs = jnp.pad(x_sample.reshape(nb, D_MODEL), ((0, m_s - nb), (0, 0)))
    pp = p_prompt.reshape(DEPTH, m_p, P_DIM)
    ps = jnp.pad(p_sample.reshape(DEPTH, nb, P_DIM), ((0, 0), (0, m_s - nb), (0, 0)))

    def ffn(stream, w_gu, w_down, next_gain, i):
        xp, xs, gp, gs, qp, qs = stream
        ap, a_s = dense("swiglu_mm", _swiglu_body, [(gp, None)], [(gs, None)],
                        [(w_gu, 0), (w_gu, D_FF)], i, D_FF, BF16, norm_in=(qp, qs))
        return dense("ffn_down", functools.partial(_residual_body, scale=0.5),
                     [(ap, None), (xp, 0)], [(a_s, None), (xs, 0)], [(w_down, 0)], i, D_MODEL, F32,
                     norm_out=(next_gain, i))

    stream = (xp, xs) + tuple(prenorm(xp, xs, ffn1_norm, 0, tp(512)))
    outs = [[] for _ in range(6)]
    kv_p = None
    for i in range(DEPTH):
        stream = ffn(stream, ffn1_w_gu, ffn1_w_down, mix_norm, i)
        xp, xs, gp, gs, qp, qs = stream
        proj, proj_s = dense("in_proj", _proj_body, [(gp, None)], [(gs, None)], [(w_in, 0)], i, N_IN, F32,
                             norm_in=(qp, qs))

        ya, s_p = retention_prompt(proj, cos_p, sin_p, lg, ret_gn, i, batch, seq)
        yb, c_p = conv_prompt(proj, conv_w, i, batch, seq)
        kv_p = new_kv(proj, sb_k_norm, i, kv_p, tp(512))
        yc = sb_prompt(proj, kv_p[0], sb_q_norm, sb_bias[i], i, batch, seq)

        ya_s, yb_s, sq_s, sk_s, s_s, c_s = decode_mix(
            proj_s[:nb].reshape(nb, 1, N_IN), state_ret, state_conv, cos_s, sin_s, lg, ret_gn, conv_w,
            sb_q_norm, sb_k_norm, i, nb)
        q_rows = jnp.pad(sq_s.reshape(nb, HEADS, HEAD_DIM), row_pad)
        yc_s = sb_decode(q_rows, bias_cols, cache_sb_k, cache_sb_v, page_table, i)
        yc_s = yc_s.reshape(nb, MIX_W).astype(BF16)
        pad_rows = lambda y: jnp.pad(y.reshape(nb, MIX_W), ((0, m_s - nb), (0, 0)))
        sv_s = proj_s[:nb, COL_SV * MIX_W:(COL_SV + 1) * MIX_W]

        def branch_rows(ya, yb, yc, proj):
            return [(ya, None), (yb, None), (yc, None),
                    (proj, COL_GA * MIX_W), (proj, COL_GB * MIX_W), (proj, COL_GC * MIX_W)]

        mp, ms = dense("merge_branches", _merge_body, branch_rows(ya, yb, yc, proj),
                       branch_rows(pad_rows(ya_s), pad_rows(yb_s), pad_rows(yc_s), proj_s),
                       [(w_branch_ret, 0), (w_branch_conv, 0), (w_branch_sb, 0)], i, D_MODEL, BF16)
        stream = dense("out_proj", functools.partial(_residual_body, scale=1.0),
                       [(mp, None), (xp, 0)], [(ms, None), (xs, 0)], [(w_out, 0)], i, D_MODEL, F32,
                       norm_out=(ffn2_norm, i))
        xp, xs, gp, gs, qp, qs = ffn(stream, ffn2_w_gu, ffn2_w_down, ple_norm, i)
        stream = dense("ple_update", _ple_body, [(gp, None), (pp[i], None), (xp, 0)],
                       [(gs, None), (ps[i], None), (xs, 0)], [(w_ple_gate, 0), (w_ple_up, 0)],
                       i, D_MODEL, F32, norm_in=(qp, qs),
                       norm_out=(ffn1_norm, i + 1) if i + 1 < DEPTH else None)
        xp, xs = stream[:2]

        for lst, val in zip(outs, (sk_s.reshape(nb, 1, HEADS, HEAD_DIM),
                                   sv_s.reshape(nb, 1, HEADS, HEAD_DIM),
                                   s_p, s_s, c_p, c_s)):
            lst.append(val)

    new_k_p, new_v_p = (a.reshape(DEPTH, batch, seq, HEADS, HEAD_DIM) for a in kv_p)
    return ((xp.reshape(batch, seq, D_MODEL), xs[:nb].reshape(nb, 1, D_MODEL), new_k_p, new_v_p)
            + tuple(jnp.stack(o) for o in outs))
```
